```python
import math
import jax
import jax.numpy as jnp
from jax import lax
import numpy as np

D_MODEL = 1024
BATCH = 1
SEQ = 16384
DEPTH = 4

N_EVEN = (DEPTH + 1) // 2
N_ODD = DEPTH // 2
EPS = 1e-6
TINY = 1e-20
D_FF = 2816

HG_DK = 128
HG_DV = 128
HG_HEADS = D_MODEL // HG_DK
HG_WIDTH = HG_HEADS * HG_DV
HG_CHUNK = 64

SSD_INNER = D_MODEL
SSD_P = 64
SSD_HEADS = SSD_INNER // SSD_P
SSD_N = 128
SSD_G = 4
SSD_R = SSD_HEADS // SSD_G
SSD_CONV = 4
SSD_CHUNK = 128
SSD_XBC = SSD_INNER + 2 * SSD_G * SSD_N
DT_MIN = 1e-3
DT_MAX = 1e-1

HYB_SIZES = (HG_HEADS * HG_DK, HG_HEADS * HG_DK, HG_WIDTH, HG_WIDTH, SSD_INNER, SSD_XBC, SSD_HEADS)
HYB_IN = sum(HYB_SIZES)
HYB_SPLITS = tuple(sum(HYB_SIZES[:i + 1]) for i in range(len(HYB_SIZES) - 1))
HYB_OUT = HG_WIDTH + SSD_INNER

FOX_HD = 64
FOX_HEADS = D_MODEL // FOX_HD
FOX_WIDTH = FOX_HEADS * FOX_HD
FOX_SIZES = (FOX_WIDTH, FOX_WIDTH, FOX_WIDTH, FOX_HEADS)
FOX_IN = sum(FOX_SIZES)
FOX_SPLITS = tuple(sum(FOX_SIZES[:i + 1]) for i in range(len(FOX_SIZES) - 1))
Q_BLOCK = 128
NEG_INF = -1e30

kernel_name = "hybrid_hgrn2_ssd_fox_macaron"


def rmsnorm(x, g):
    xf = x.astype(jnp.float32)
    y = xf * lax.rsqrt(jnp.mean(xf * xf, axis=-1, keepdims=True) + EPS)
    return (y * g.astype(jnp.float32)).astype(x.dtype)


def swiglu(u, w_gate, w_up, w_down):
    return (jax.nn.silu(u @ w_gate) * (u @ w_up)) @ w_down


def causal_dwconv(u, w, b):
    k_w = w.shape[0]
    out = lax.conv_general_dilated(u, w.astype(u.dtype)[:, None, :], window_strides=(1,),
                                   padding=[(k_w - 1, 0)],
                                   dimension_numbers=('NWC', 'WIO', 'NWC'),
                                   feature_group_count=u.shape[-1])
    return out + b.astype(u.dtype)


def seg_decay(a):
    n = a.shape[-1]
    cs = jnp.cumsum(a, axis=-1)
    diff = cs[..., :, None] - cs[..., None, :]
    mask = jnp.tril(jnp.ones((n, n), dtype=bool))
    return jnp.where(mask, jnp.exp(jnp.where(mask, diff, 0.0)), 0.0)


def hgrn2_mixer(q, f_logit, inp, g, lb, out_gain):
    bsz, s, _ = q.shape
    nc = s // HG_CHUNK
    lb = lb.astype(jnp.float32)
    f = lb + (1.0 - lb) * jax.nn.sigmoid(f_logit)
    log_f = jnp.log(jnp.maximum(f, TINY))
    k = (1.0 - lb) * jax.nn.sigmoid(-f_logit)

    def chunks(t, d):
        return t.reshape(bsz, nc, HG_CHUNK, HG_HEADS, d).transpose(1, 0, 3, 2, 4)

    causal = jnp.tril(jnp.ones((HG_CHUNK, HG_CHUNK), dtype=bool))[:, :, None]

    def step(state, blk):
        qb, kb, lfb, vb = blk
        b = jnp.cumsum(lfb, axis=2)
        diff = b[:, :, :, None, :] - b[:, :, None, :, :]
        decay = jnp.where(causal, jnp.exp(jnp.where(causal, diff, 0.0)), 0.0)
        scores = jnp.einsum('bhtd,bhtsd,bhsd->bhts', qb, decay, kb)
        o = (jnp.einsum('bhts,bhsv->bhtv', scores, vb)
             + jnp.einsum('bhtd,bhdv->bhtv', qb * jnp.exp(b), state))
        b_last = b[:, :, -1:, :]
        state = (jnp.exp(b_last[:, :, 0, :])[..., None] * state
                 + jnp.einsum('bhsd,bhsv->bhdv', kb * jnp.exp(b_last - b), vb))
        return state, o

    state0 = jnp.zeros((bsz, HG_HEADS, HG_DK, HG_DV), jnp.float32)
    _, o = lax.scan(step, state0, (chunks(q, HG_DK), chunks(k, HG_DK), chunks(log_f, HG_DK), chunks(inp, HG_DV)))
    o = o.transpose(1, 0, 3, 2, 4).reshape(bsz, s, HG_HEADS, HG_DV)
    o = o * lax.rsqrt(jnp.mean(o * o, axis=-1, keepdims=True) + EPS)
    return o.reshape(bsz, s, HG_WIDTH) * out_gain.astype(jnp.float32) * jax.nn.silu(g)


def ssd_mixer(z, xbc, dt_raw, conv_w, conv_b, dt_bias, a_log, d_skip, out_gain):
    bsz, s, _ = z.shape
    nc = s // SSD_CHUNK
    f32 = jnp.float32
    xbc = jax.nn.silu(causal_dwconv(xbc, conv_w.astype(f32), conv_b.astype(f32)))
    xs, bm, cm = jnp.split(xbc, (SSD_INNER, SSD_INNER + SSD_G * SSD_N), axis=-1)
    dt = jax.nn.softplus(dt_raw + dt_bias.astype(f32))
    a_head = -jnp.exp(a_log.astype(f32)).reshape(SSD_G, SSD_R)

    x_c = xs.reshape(bsz, nc, SSD_CHUNK, SSD_G, SSD_R, SSD_P)
    b_c = bm.reshape(bsz, nc, SSD_CHUNK, SSD_G, SSD_N)
    c_c = cm.reshape(bsz, nc, SSD_CHUNK, SSD_G, SSD_N)
    dt_c = dt.reshape(bsz, nc, SSD_CHUNK, SSD_G, SSD_R)
    xdt = x_c * dt_c[..., None]
    a = (dt_c * a_head).transpose(0, 3, 4, 1, 2)
    a_cs = jnp.cumsum(a, axis=-1)

    cb = jnp.einsum('bclgn,bcsgn->bgcls', c_c, b_c)
    m = cb[:, :, None] * seg_decay(a)
    y_diag = jnp.einsum('bgrcls,bcsgrp->bclgrp', m, xdt)

    decay_states = jnp.exp(a_cs[..., -1:] - a_cs)
    states = jnp.einsum('bclgn,bgrcl,bclgrp->bcgrpn', b_c, decay_states, xdt)
    states = jnp.concatenate([jnp.zeros_like(states[:, :1]), states], axis=1)
    chunk_tot = jnp.pad(a_cs[..., -1], ((0, 0), (0, 0), (0, 0), (1, 0)))
    decay_chunk = seg_decay(chunk_tot)
    states = jnp.einsum('bgrzc,bcgrpn->bzgrpn', decay_chunk, states)[:, :-1]
    y_off = jnp.einsum('bclgn,bcgrpn,bgrcl->bclgrp', c_c, states, jnp.exp(a_cs))

    y = (y_diag + y_off).reshape(bsz, s, SSD_HEADS, SSD_P)
    y = y + xs.reshape(bsz, s, SSD_HEADS, SSD_P) * d_skip.astype(f32)[:, None]
    y = y.reshape(bsz, s, SSD_INNER) * jax.nn.silu(z)
    y = y * lax.rsqrt(jnp.mean(y * y, axis=-1, keepdims=True) + EPS)
    return y * out_gain.astype(f32)


def hybrid_hgrn2_ssd(u, w_in, lb, hg_gain, conv_w, conv_b, dt_bias, a_log, d_skip, ssd_gain, w_out):
    proj = (u @ w_in).astype(jnp.float32)
    q, f_logit, inp, g, z, xbc, dt_raw = jnp.split(proj, HYB_SPLITS, axis=-1)
    o_a = hgrn2_mixer(q, f_logit, inp, g, lb, hg_gain)
    o_b = ssd_mixer(z, xbc, dt_raw, conv_w, conv_b, dt_bias, a_log, d_skip, ssd_gain)
    o = jnp.concatenate([o_a, o_b], axis=-1).astype(u.dtype)
    return o @ w_out


def fox_attention(u, w_in, b_f, w_out):
    bsz, s, _ = u.shape
    f32 = jnp.float32
    proj = (u @ w_in).astype(f32)
    q, k, v, f_logit = jnp.split(proj, FOX_SPLITS, axis=-1)

    def heads(t):
        return t.reshape(bsz, s, FOX_HEADS, FOX_HD).transpose(0, 2, 1, 3)

    q = heads(q) * (FOX_HD ** -0.5)
    k = heads(k)
    v = heads(v)
    log_f = jax.nn.log_sigmoid(f_logit + b_f.astype(f32))
    cum = jnp.cumsum(log_f, axis=1).transpose(0, 2, 1)
    nb = s // Q_BLOCK
    q_blocks = q.reshape(bsz, FOX_HEADS, nb, Q_BLOCK, FOX_HD).transpose(2, 0, 1, 3, 4)
    cum_blocks = cum.reshape(bsz, FOX_HEADS, nb, Q_BLOCK).transpose(2, 0, 1, 3)
    k_pos = jnp.arange(s)

    def block(args):
        qi, ci, start = args
        logits = jnp.einsum('bhqd,bhkd->bhqk', qi, k) + ci[..., None] - cum[:, :, None, :]
        q_pos = start + jnp.arange(Q_BLOCK)
        logits = jnp.where(k_pos[None, :] <= q_pos[:, None], logits, NEG_INF)
        p = jax.nn.softmax(logits, axis=-1)
        return jnp.einsum('bhqk,bhkd->bhqd', p, v)

    o = lax.map(block, (q_blocks, cum_blocks, jnp.arange(nb) * Q_BLOCK))
    o = o.transpose(1, 0, 3, 2, 4).reshape(bsz, s, FOX_WIDTH).astype(u.dtype)
    return o @ w_out


def setup_inputs(seed: int = 0) -> dict:
    key = jax.random.key(seed)
    ks = jax.random.split(key, 24)
    f32 = jnp.float32

    def nrm(k, shape, fan_in):
        return jax.random.normal(k, shape, f32) * (fan_in ** -0.5)

    def gain(k, shape):
        return 1.0 + 0.02 * jax.random.normal(k, shape, f32)

    dt = jnp.exp(jax.random.uniform(ks[11], (N_EVEN, SSD_HEADS), f32, math.log(DT_MIN), math.log(DT_MAX)))
    return {
        "x": jax.random.normal(ks[0], (BATCH, SEQ, D_MODEL), f32),
        "ffn1_norm": gain(ks[1], (DEPTH, D_MODEL)),
        "ffn1_w_gate": nrm(ks[2], (DEPTH, D_MODEL, D_FF), D_MODEL),
        "ffn1_w_up": nrm(ks[3], (DEPTH, D_MODEL, D_FF), D_MODEL),
        "ffn1_w_down": nrm(ks[4], (DEPTH, D_FF, D_MODEL), D_FF),
        "mix_norm": gain(ks[5], (DEPTH, D_MODEL)),
        "hyb_w_in": nrm(ks[6], (N_EVEN, D_MODEL, HYB_IN), D_MODEL),
        "hgrn_lb_logits": 0.5 * jax.random.normal(ks[7], (N_EVEN, HG_HEADS * HG_DK), f32),
        "hgrn_out_gain": gain(ks[8], (N_EVEN, HG_WIDTH)),
        "ssd_conv_w": nrm(ks[9], (N_EVEN, SSD_CONV, SSD_XBC), SSD_CONV),
        "ssd_conv_b": 0.02 * jax.random.normal(ks[10], (N_EVEN, SSD_XBC), f32),
        "ssd_dt_bias": dt + jnp.log(-jnp.expm1(-dt)),
        "ssd_A_log": jnp.log(jax.random.uniform(ks[12], (N_EVEN, SSD_HEADS), f32, 1.0, 16.0)),
        "ssd_D": gain(ks[13], (N_EVEN, SSD_HEADS)),
        "ssd_out_gain": gain(ks[14], (N_EVEN, SSD_INNER)),
        "hyb_w_out": nrm(ks[15], (N_EVEN, HYB_OUT, D_MODEL), HYB_OUT),
        "fox_w_in": nrm(ks[16], (N_ODD, D_MODEL, FOX_IN), D_MODEL),
        "fox_b_f": jax.random.uniform(ks[17], (N_ODD, FOX_HEADS), f32, 1.0, 6.0),
        "fox_w_out": nrm(ks[18], (N_ODD, FOX_WIDTH, D_MODEL), FOX_WIDTH),
        "ffn2_norm": gain(ks[19], (DEPTH, D_MODEL)),
        "ffn2_w_gate": nrm(ks[20], (DEPTH, D_MODEL, D_FF), D_MODEL),
        "ffn2_w_up": nrm(ks[21], (DEPTH, D_MODEL, D_FF), D_MODEL),
        "ffn2_w_down": nrm(ks[22], (DEPTH, D_FF, D_MODEL), D_FF),
        "final_norm": gain(ks[23], (D_MODEL,)),
    }


def reference(x, ffn1_norm, ffn1_w_gate, ffn1_w_up, ffn1_w_down, mix_norm,
              hyb_w_in, hgrn_lb_logits, hgrn_out_gain, ssd_conv_w, ssd_conv_b,
              ssd_dt_bias, ssd_A_log, ssd_D, ssd_out_gain, hyb_w_out,
              fox_w_in, fox_b_f, fox_w_out,
              ffn2_norm, ffn2_w_gate, ffn2_w_up, ffn2_w_down, final_norm):
    p = jax.nn.softmax(hgrn_lb_logits.astype(jnp.float32), axis=0)
    lower_bounds = jnp.cumsum(p, axis=0) - p[0]
    h = x
    for layer in range(DEPTH):
        j = layer // 2
        u = rmsnorm(h, ffn1_norm[layer])
        h = h + 0.5 * swiglu(u, ffn1_w_gate[layer], ffn1_w_up[layer], ffn1_w_down[layer])
        u = rmsnorm(h, mix_norm[layer])
        if layer % 2 == 0:
            mix = hybrid_hgrn2_ssd(u, hyb_w_in[j], lower_bounds[j], hgrn_out_gain[j],
                                   ssd_conv_w[j], ssd_conv_b[j], ssd_dt_bias[j], ssd_A_log[j],
                                   ssd_D[j], ssd_out_gain[j], hyb_w_out[j])
        else:
            mix = fox_attention(u, fox_w_in[j], fox_b_f[j], fox_w_out[j])
        h = h + mix.astype(h.dtype)
        u = rmsnorm(h, ffn2_norm[layer])
        h = h + 0.5 * swiglu(u, ffn2_w_gate[layer], ffn2_w_up[layer], ffn2_w_down[layer])
    return rmsnorm(h, final_norm)
```

```python
import functools

import jax
import jax.numpy as jnp
from jax import lax
from jax.experimental import pallas as pl
from jax.experimental.pallas import tpu as pltpu

F32 = jnp.float32
BF16 = jnp.bfloat16

EPS = 1e-6
TINY = 1e-20
NEG_INF = -1e30

LANES = 128
VMEM_LIMIT = 56 * 1024 * 1024

HG_DK = 128
HG_SUB = 16
SSD_P = 64
SSD_N = 128
SSD_G = 4
SSD_CHUNK = 128
SSD_CONV = 4
FOX_HD = 64
CUM_PIECES = 3


def _params(*sem):
    return pltpu.CompilerParams(dimension_semantics=sem, vmem_limit_bytes=VMEM_LIMIT)


def _rms_scale(x):
    return lax.rsqrt(jnp.mean(x * x, axis=-1, keepdims=True) + EPS)


def _sigmoid(x):
    return 1.0 / (1.0 + jnp.exp(-x))


def _cumsum_rows(x, period):
    row = lax.broadcasted_iota(jnp.int32, x.shape, 0) % period
    shift = 1
    while shift < period:
        x = x + jnp.where(row >= shift, pltpu.roll(x, shift, 0), 0.0)
        shift *= 2
    return x


def _ffn_kernel(h_ref, g_ref, wg_ref, wu_ref, wd_ref, fg_ref, o_ref, u_ref, acc_ref, *, final_norm):
    j = pl.program_id(1)

    @pl.when(j == 0)
    def _():
        x = h_ref[...]
        u_ref[...] = (x * _rms_scale(x) * g_ref[...]).astype(BF16)
        acc_ref[...] = jnp.zeros_like(acc_ref)

    u = u_ref[...]
    gate = jnp.dot(u, wg_ref[...], preferred_element_type=F32)
    up = jnp.dot(u, wu_ref[...], preferred_element_type=F32)
    act = (gate * _sigmoid(gate) * up).astype(BF16)
    acc_ref[...] += jnp.dot(act, wd_ref[...], preferred_element_type=F32)

    @pl.when(j == pl.num_programs(1) - 1)
    def _():
        y = h_ref[...] + 0.5 * acc_ref[...]
        if final_norm:
            y = y * _rms_scale(y) * fg_ref[...]
        o_ref[...] = y


def _ffn(h, gain, wg, wu, wd, final_gain=None, *, tm=1024, tf=256):
    s, d = h.shape
    ff = wg.shape[1]
    assert s % tm == 0 and ff % tf == 0
    final_norm = final_gain is not None
    fg = final_gain if final_norm else gain
    return pl.pallas_call(
        functools.partial(_ffn_kernel, final_norm=final_norm),
        grid=(s // tm, ff // tf),
        in_specs=[
            pl.BlockSpec((tm, d), lambda i, j: (i, 0)),
            pl.BlockSpec((1, d), lambda i, j: (0, 0)),
            pl.BlockSpec((d, tf), lambda i, j: (0, j)),
            pl.BlockSpec((d, tf), lambda i, j: (0, j)),
            pl.BlockSpec((tf, d), lambda i, j: (j, 0)),
            pl.BlockSpec((1, d), lambda i, j: (0, 0)),
        ],
        out_specs=pl.BlockSpec((tm, d), lambda i, j: (i, 0)),
        out_shape=jax.ShapeDtypeStruct((s, d), F32),
        scratch_shapes=[pltpu.VMEM((tm, d), BF16), pltpu.VMEM((tm, d), F32)],
        compiler_params=_params("parallel", "arbitrary"),
        name="ffn",
    )(h, gain.reshape(1, d), wg, wu, wd, fg.reshape(1, d))


def _proj_kernel(h_ref, g_ref, w_ref, ws_ref, o_ref, os_ref, u_ref):
    j = pl.program_id(1)

    @pl.when(j == 0)
    def _():
        x = h_ref[...]
        u = (x * _rms_scale(x) * g_ref[...]).astype(BF16)
        u_ref[...] = u
        os_ref[...] = jnp.dot(u, ws_ref[...], preferred_element_type=F32)

    o_ref[...] = jnp.dot(u_ref[...], w_ref[...], preferred_element_type=F32)


def _norm_proj(h, gain, w_main, w_small, *, tm=1024, tn=1024):
    s, d = h.shape
    n = w_main.shape[1]
    assert s % tm == 0 and n % tn == 0
    return pl.pallas_call(
        _proj_kernel,
        grid=(s // tm, n // tn),
        in_specs=[
            pl.BlockSpec((tm, d), lambda i, j: (i, 0)),
            pl.BlockSpec((1, d), lambda i, j: (0, 0)),
            pl.BlockSpec((d, tn), lambda i, j: (0, j)),
            pl.BlockSpec((d, LANES), lambda i, j: (0, 0)),
        ],
        out_specs=[
            pl.BlockSpec((tm, tn), lambda i, j: (i, j)),
            pl.BlockSpec((tm, LANES), lambda i, j: (i, 0)),
        ],
        out_shape=[jax.ShapeDtypeStruct((s, n), F32), jax.ShapeDtypeStruct((s, LANES), F32)],
        scratch_shapes=[pltpu.VMEM((tm, d), BF16)],
        compiler_params=_params("parallel", "arbitrary"),
        name="norm_proj",
    )(h, gain.reshape(1, d), w_main, w_small)


def _out_proj_kernel(*refs, n_in):
    h_ref = refs[0]
    x_refs = refs[1:1 + n_in]
    w_refs = refs[1 + n_in:1 + 2 * n_in]
    o_ref = refs[1 + 2 * n_in]
    y = h_ref[...]
    for x_ref, w_ref in zip(x_refs, w_refs):
        y = y + jnp.dot(x_ref[...], w_ref[...], preferred_element_type=F32)
    o_ref[...] = y


def _out_proj(h, xs, ws, *, tm=1024):
    s, d = h.shape
    n_in = len(xs)
    in_specs = [pl.BlockSpec((tm, d), lambda i: (i, 0))]
    in_specs += [pl.BlockSpec((tm, x.shape[1]), lambda i: (i, 0)) for x in xs]
    in_specs += [pl.BlockSpec(w.shape, lambda i: (0, 0)) for w in ws]
    return pl.pallas_call(
        functools.partial(_out_proj_kernel, n_in=n_in),
        grid=(s // tm,),
        in_specs=in_specs,
        out_specs=pl.BlockSpec((tm, d), lambda i: (i, 0)),
        out_shape=jax.ShapeDtypeStruct((s, d), F32),
        compiler_params=_params("parallel"),
        name="out_proj",
    )(h, *xs, *ws)


def _hgrn_kernel(q_ref, f_ref, v_ref, g_ref, lbl_ref, gain_ref, o_ref,
                 state_ref, kv_ref, sprev_ref, *, layer_j, tb):
    c = HG_SUB
    nb = tb // c

    @pl.when(pl.program_id(1) == 0)
    def _():
        state_ref[...] = jnp.zeros_like(state_ref)

    logits = lbl_ref[...]
    e = jnp.exp(logits - jnp.max(logits, axis=0, keepdims=True))
    p = e / jnp.sum(e, axis=0, keepdims=True)
    lb = jnp.zeros((1, HG_DK), F32)
    for i in range(1, layer_j + 1):
        lb = lb + p[i:i + 1]

    fl = f_ref[...]
    f = lb + (1.0 - lb) * _sigmoid(fl)
    logf = jnp.log(jnp.maximum(f, TINY))
    k = (1.0 - lb) * _sigmoid(-fl)
    q = q_ref[...]
    v = v_ref[...]

    b = _cumsum_rows(logf, c)
    b3 = b.reshape(nb, c, HG_DK)
    q3 = q.reshape(nb, c, HG_DK)
    k3 = k.reshape(nb, c, HG_DK)
    v3 = v.reshape(nb, c, HG_DK)
    b_last = b3[:, c - 1:c, :]

    kd = (k3 * jnp.exp(b_last - b3)).astype(BF16)
    kv_ref[...] = lax.dot_general(v3.astype(BF16), kd, (((1,), (1,)), ((0,), (0,))),
                                  preferred_element_type=F32)
    dec = jnp.exp(b_last)

    s_cur = state_ref[...]
    for n in range(nb):
        sprev_ref[n] = s_cur.astype(BF16)
        s_cur = s_cur * dec[n] + kv_ref[n]
    state_ref[...] = s_cur

    qd = (q3 * jnp.exp(b3)).astype(BF16)
    o = lax.dot_general(qd, sprev_ref[...], (((2,), (2,)), ((0,), (0,))),
                        preferred_element_type=F32)

    row = lax.broadcasted_iota(jnp.int32, (nb, c, 1), 1)
    for s in range(c):
        live = row >= s
        diff = b3 - b3[:, s:s + 1, :]
        w = jnp.exp(jnp.where(live, diff, 0.0))
        col = jnp.sum(q3 * w * k3[:, s:s + 1, :], axis=-1, keepdims=True)
        o = o + jnp.where(live, col, 0.0) * v3[:, s:s + 1, :]

    o = o.reshape(tb, HG_DK)
    g = g_ref[...]
    o_ref[...] = (o * _rms_scale(o) * gain_ref[...] * (g * _sigmoid(g))).astype(o_ref.dtype)


def _hgrn(proj, lb_logits, gain, layer_j, *, tb=512):
    s = proj.shape[0]
    heads = gain.shape[0] // HG_DK
    n_even = lb_logits.shape[0]
    nb = tb // HG_SUB
    blk = lambda off: pl.BlockSpec((tb, HG_DK), lambda h, t, off=off: (t, off + h))
    return pl.pallas_call(
        functools.partial(_hgrn_kernel, layer_j=layer_j, tb=tb),
        grid=(heads, s // tb),
        in_specs=[
            blk(0), blk(heads), blk(2 * heads), blk(3 * heads),
            pl.BlockSpec((n_even, HG_DK), lambda h, t: (0, h)),
            pl.BlockSpec((1, HG_DK), lambda h, t: (0, h)),
        ],
        out_specs=pl.BlockSpec((tb, HG_DK), lambda h, t: (t, h)),
        out_shape=jax.ShapeDtypeStruct((s, heads * HG_DK), BF16),
        scratch_shapes=[
            pltpu.VMEM((HG_DK, HG_DK), F32),
            pltpu.VMEM((nb, HG_DK, HG_DK), F32),
            pltpu.VMEM((nb, HG_DK, HG_DK), BF16),
        ],
        compiler_params=_params("parallel", "arbitrary"),
        name="hgrn2",
    )(proj, proj, proj, proj, lb_logits, gain.reshape(1, -1))


def _split_bf16(x, parts):
    out = []
    for _ in range(parts):
        hi = x.astype(BF16)
        out.append(hi)
        x = x - hi.astype(F32)
    return out


def _ssd_kernel(z_ref, xs_ref, bc_ref, xsh_ref, bch_ref, dt_ref, cwx_ref, cwbc_ref, cbx_ref, cbbc_ref,
                dtb_ref, alog_ref, dskip_ref, gain_ref, expand_ref, o_ref, state_ref):
    L = SSD_CHUNK
    inner = z_ref.shape[1]
    heads = inner // SSD_P
    t = pl.program_id(0)

    @pl.when(t == 0)
    def _():
        state_ref[...] = jnp.zeros_like(state_ref)

    halo_on = (t > 0).astype(F32)

    def conv_silu(x_ref, halo_ref, w_ref, b_ref):
        x = x_ref[...]
        full = jnp.concatenate([halo_ref[...] * halo_on, x], axis=0)
        w = w_ref[...]
        acc = b_ref[...] + w[SSD_CONV - 1:SSD_CONV] * x
        for kk in range(SSD_CONV - 1):
            off = 8 - (SSD_CONV - 1) + kk
            acc = acc + w[kk:kk + 1] * full[off:off + L]
        return acc * _sigmoid(acc)

    xs = conv_silu(xs_ref, xsh_ref, cwx_ref, cbx_ref)
    bc = conv_silu(bc_ref, bch_ref, cwbc_ref, cbbc_ref)
    bm = bc[:, :SSD_G * SSD_N]
    cm = bc[:, SSD_G * SSD_N:]

    pre = dt_ref[...] + dtb_ref[...]
    dt = jnp.maximum(pre, 0.0) + jnp.log(1.0 + jnp.exp(-jnp.abs(pre)))
    a = dt * (-jnp.exp(alog_ref[...]))
    cs = _cumsum_rows(a, L)
    cs_t = cs.T

    expand = expand_ref[...]

    def spread(x, parts):
        y = jnp.zeros((x.shape[0], inner), F32)
        for piece in _split_bf16(x, parts):
            y = y + jnp.dot(piece, expand, preferred_element_type=F32)
        return y

    dt_e = spread(dt, 3)
    cs_e = spread(cs, 3)
    cs_last_e = cs_e[L - 1:L, :]
    xdt = xs * dt_e
    xdt_b = xdt.astype(BF16)
    w_state = (xdt * jnp.exp(cs_last_e - cs_e)).astype(BF16)
    grow = jnp.exp(cs_e)
    dec_e = jnp.exp(cs_last_e)

    tril = (lax.broadcasted_iota(jnp.int32, (L, L), 0) >= lax.broadcasted_iota(jnp.int32, (L, L), 1))
    lane = lax.broadcasted_iota(jnp.int32, (1, 2 * SSD_P), 1)
    rpg = heads // SSD_G
    gw = rpg * SSD_P
    y_parts = []
    for g in range(SSD_G):
        cg = cm[:, g * SSD_N:(g + 1) * SSD_N].astype(BF16)
        bg = bm[:, g * SSD_N:(g + 1) * SSD_N].astype(BF16)
        cb = lax.dot_general(cg, bg, (((1,), (1,)), ((), ())), preferred_element_type=F32)
        st = state_ref[:, g * gw:(g + 1) * gw]
        y_off = jnp.dot(cg, st.astype(BF16), preferred_element_type=F32)
        new = jnp.dot(bg.T, w_state[:, g * gw:(g + 1) * gw], preferred_element_type=F32)
        state_ref[:, g * gw:(g + 1) * gw] = st * dec_e[:, g * gw:(g + 1) * gw] + new
        for pr in range(rpg // 2):
            c0 = g * gw + pr * 2 * SSD_P
            x_pair = xdt_b[:, c0:c0 + 2 * SSD_P]
            y_pair = jnp.zeros((L, 2 * SSD_P), F32)
            for half in range(2):
                h = g * rpg + pr * 2 + half
                diff = cs[:, h:h + 1] - cs_t[h:h + 1, :]
                decay = jnp.where(tril, jnp.exp(jnp.where(tril, diff, 0.0)), 0.0)
                m = (cb * decay).astype(BF16)
                keep = (lane >= SSD_P) if half else (lane < SSD_P)
                x_half = jnp.where(keep, x_pair, jnp.zeros_like(x_pair))
                y_pair = y_pair + jnp.dot(m, x_half, preferred_element_type=F32)
            y_parts.append(y_pair + y_off[:, pr * 2 * SSD_P:(pr + 1) * 2 * SSD_P]
                           * grow[:, c0:c0 + 2 * SSD_P])
    y = jnp.concatenate(y_parts, axis=-1)
    y = y + xs * dskip_ref[...]
    z = z_ref[...]
    y = y * (z * _sigmoid(z))
    o_ref[...] = (y * _rms_scale(y) * gain_ref[...]).astype(o_ref.dtype)


def _ssd(proj, dt_raw, conv_w, conv_b, dt_bias, a_log, d_skip, gain, z_blk, xs_blk):
    s = proj.shape[0]
    inner = gain.shape[0]
    heads = inner // SSD_P
    L = SSD_CHUNK
    assert 2 * SSD_G * SSD_N == inner and heads <= LANES

    def pad_heads(x):
        return jnp.pad(x.astype(F32), (0, LANES - heads)).reshape(1, LANES)

    expand = (jnp.arange(LANES)[:, None] == (jnp.arange(inner)[None, :] // SSD_P)).astype(BF16)
    halo = lambda blk: pl.BlockSpec((8, inner), lambda t, blk=blk: (jnp.maximum(t * (L // 8) - 1, 0), blk))
    main = lambda blk: pl.BlockSpec((L, inner), lambda t, blk=blk: (t, blk))
    row = lambda width: pl.BlockSpec((1, width), lambda t: (0, 0))
    return pl.pallas_call(
        _ssd_kernel,
        grid=(s // L,),
        in_specs=[
            main(z_blk), main(xs_blk), main(xs_blk + 1), halo(xs_blk), halo(xs_blk + 1),
            pl.BlockSpec((L, LANES), lambda t: (t, 0)),
            pl.BlockSpec((SSD_CONV, inner), lambda t: (0, 0)),
            pl.BlockSpec((SSD_CONV, inner), lambda t: (0, 1)),
            pl.BlockSpec((1, inner), lambda t: (0, 0)),
            pl.BlockSpec((1, inner), lambda t: (0, 1)),
            row(LANES), row(LANES), row(inner), row(inner),
            pl.BlockSpec((LANES, inner), lambda t: (0, 0)),
        ],
        out_specs=pl.BlockSpec((L, inner), lambda t: (t, 0)),
        out_shape=jax.ShapeDtypeStruct((s, inner), BF16),
        scratch_shapes=[pltpu.VMEM((SSD_N, inner), F32)],
        compiler_params=_params("arbitrary"),
        name="ssd",
    )(proj, proj, proj, proj, proj, dt_raw, conv_w, conv_w, conv_b.reshape(1, -1), conv_b.reshape(1, -1),
      pad_heads(dt_bias), pad_heads(a_log), jnp.repeat(d_skip.astype(F32), SSD_P).reshape(1, inner),
      gain.reshape(1, inner), expand)


def _logf_cumsum_kernel(f_ref, b_ref, *refs):
    o_refs, carry_ref = refs[:-1], refs[-1]

    @pl.when(pl.program_id(0) == 0)
    def _():
        carry_ref[...] = jnp.zeros_like(carry_ref)

    x = f_ref[...] + b_ref[...]
    ls = jnp.minimum(x, 0.0) - jnp.log(1.0 + jnp.exp(-jnp.abs(x)))
    cum = _cumsum_rows(ls, x.shape[0]) + carry_ref[...]
    carry_ref[...] = cum[x.shape[0] - 1:, :]
    for o_ref, piece in zip(o_refs, _split_bf16(cum, len(o_refs))):
        o_ref[...] = piece


def _logf_cumsum(f_logit, b_f, *, tb=512):
    s = f_logit.shape[0]
    return pl.pallas_call(
        _logf_cumsum_kernel,
        grid=(s // tb,),
        in_specs=[pl.BlockSpec((tb, LANES), lambda t: (t, 0)), pl.BlockSpec((1, LANES), lambda t: (0, 0))],
        out_specs=[pl.BlockSpec((tb, LANES), lambda t: (t, 0))] * CUM_PIECES,
        out_shape=[jax.ShapeDtypeStruct((s, LANES), BF16)] * CUM_PIECES,
        scratch_shapes=[pltpu.VMEM((1, LANES), F32)],
        compiler_params=_params("arbitrary"),
        name="logf_cumsum",
    )(f_logit, b_f)


def _attn_kernel(q_ref, k_ref, v_ref, o_ref, m_ref, acc_ref, *, tq, tk):
    i = pl.program_id(1)
    q = q_ref[0]
    m_ref[...] = jnp.full_like(m_ref, NEG_INF)
    acc_ref[...] = jnp.zeros_like(acc_ref)

    def step(j, masked):
        start = pl.multiple_of(j * tk, tk)
        k = k_ref[0, pl.ds(start, tk), :]
        v = v_ref[0, pl.ds(start, tk), :]
        s = lax.dot_general(q, k, (((1,), (1,)), ((), ())), preferred_element_type=F32)
        if masked:
            qpos = lax.broadcasted_iota(jnp.int32, (tq, tk), 0)
            kpos = lax.broadcasted_iota(jnp.int32, (tq, tk), 1)
            s = jnp.where(kpos <= qpos, s, NEG_INF)
        m_old = m_ref[...]
        m_new = jnp.maximum(m_old, jnp.max(s, axis=-1, keepdims=True))
        p = jnp.exp(s - m_new).astype(BF16)
        acc_ref[...] = jnp.exp(m_old - m_new) * acc_ref[...] + jnp.dot(p, v, preferred_element_type=F32)
        m_ref[...] = m_new

    def body(j, carry):
        step(j, False)
        return carry

    lax.fori_loop(0, i, body, 0)
    step(i, True)
    acc = acc_ref[...]
    o_ref[0] = (acc / acc[:, FOX_HD:FOX_HD + 1]).astype(o_ref.dtype)


def _attention(q_aug, k_aug, v_aug, *, tq=512):
    heads, s, w = q_aug.shape
    tk = tq
    return pl.pallas_call(
        functools.partial(_attn_kernel, tq=tq, tk=tk),
        grid=(heads, s // tq),
        in_specs=[
            pl.BlockSpec((1, tq, w), lambda h, i: (h, i, 0)),
            pl.BlockSpec((1, s, w), lambda h, i: (h, 0, 0)),
            pl.BlockSpec((1, s, w), lambda h, i: (h, 0, 0)),
        ],
        out_specs=pl.BlockSpec((1, tq, w), lambda h, i: (h, i, 0)),
        out_shape=jax.ShapeDtypeStruct((heads, s, w), BF16),
        scratch_shapes=[pltpu.VMEM((tq, 1), F32), pltpu.VMEM((tq, w), F32)],
        compiler_params=_params("parallel", "arbitrary"),
        name="fox_attention",
    )(q_aug, k_aug, v_aug)


def _fox_operands(proj, cum_pieces, heads):
    s = proj.shape[0]
    width = heads * FOX_HD

    def to_heads(x):
        return x.reshape(s, heads, FOX_HD).transpose(1, 0, 2)

    q = (to_heads(proj[:, :width]) * (FOX_HD ** -0.5)).astype(BF16)
    k = to_heads(proj[:, width:2 * width]).astype(BF16)
    v = to_heads(proj[:, 2 * width:3 * width]).astype(BF16)
    cpos = jnp.stack([c[:, :heads].T for c in cum_pieces], axis=-1)
    ones3 = jnp.ones((heads, s, CUM_PIECES), BF16)
    zq = jnp.zeros((heads, s, LANES - FOX_HD - 2 * CUM_PIECES), BF16)
    q_aug = jnp.concatenate([q, cpos, ones3, zq], axis=-1)
    k_aug = jnp.concatenate([k, ones3, -cpos, zq], axis=-1)
    v_aug = jnp.concatenate([v, jnp.ones((heads, s, 1), BF16),
                             jnp.zeros((heads, s, LANES - FOX_HD - 1), BF16)], axis=-1)
    return q_aug, k_aug, v_aug


def _pad_cols(w, width):
    return jnp.pad(w, ((0, 0), (0, width - w.shape[1])))


def kernel(x, ffn1_norm, ffn1_w_gate, ffn1_w_up, ffn1_w_down, mix_norm, hyb_w_in, hgrn_lb_logits,
           hgrn_out_gain, ssd_conv_w, ssd_conv_b, ssd_dt_bias, ssd_A_log, ssd_D, ssd_out_gain, hyb_w_out,
           fox_w_in, fox_b_f, fox_w_out, ffn2_norm, ffn2_w_gate, ffn2_w_up, ffn2_w_down, final_norm):
    bsz, s, d = x.shape
    depth = ffn1_norm.shape[0]
    hg_width = hgrn_out_gain.shape[1]
    inner = ssd_out_gain.shape[1]
    ssd_heads = ssd_dt_bias.shape[1]
    fox_heads = fox_b_f.shape[1]
    fox_width = fox_heads * FOX_HD
    hyb_main = 4 * hg_width + inner + inner + 2 * SSD_G * SSD_N
    assert hg_width == inner == d and hyb_w_in.shape[2] == hyb_main + ssd_heads

    bf = lambda w: w.astype(BF16)
    outs = []
    for b in range(bsz):
        h = x[b]
        for layer in range(depth):
            j = layer // 2
            h = _ffn(h, ffn1_norm[layer], bf(ffn1_w_gate[layer]), bf(ffn1_w_up[layer]), bf(ffn1_w_down[layer]))
            if layer % 2 == 0:
                w_in = bf(hyb_w_in[j])
                proj, dt_raw = _norm_proj(h, mix_norm[layer], w_in[:, :hyb_main],
                                          _pad_cols(w_in[:, hyb_main:], LANES))
                o_a = _hgrn(proj, hgrn_lb_logits, hgrn_out_gain[j], j)
                o_b = _ssd(proj, dt_raw, ssd_conv_w[j], ssd_conv_b[j], ssd_dt_bias[j], ssd_A_log[j],
                           ssd_D[j], ssd_out_gain[j], z_blk=4 * hg_width // inner,
                           xs_blk=4 * hg_width // inner + 1)
                w_out = bf(hyb_w_out[j])
                h = _out_proj(h, [o_a, o_b], [w_out[:hg_width], w_out[hg_width:]])
            else:
                w_in = bf(fox_w_in[j])
                proj, f_logit = _norm_proj(h, mix_norm[layer], w_in[:, :3 * fox_width],
                                           _pad_cols(w_in[:, 3 * fox_width:], LANES))
                cum = _logf_cumsum(f_logit, jnp.pad(fox_b_f[j], (0, LANES - fox_heads)).reshape(1, LANES))
                o = _attention(*_fox_operands(proj, cum, fox_heads))
                o = o[:, :, :FOX_HD].transpose(1, 0, 2).reshape(s, fox_width)
                h = _out_proj(h, [o], [bf(fox_w_out[j])])
            last = layer == depth - 1
            h = _ffn(h, ffn2_norm[layer], bf(ffn2_w_gate[layer]), bf(ffn2_w_up[layer]), bf(ffn2_w_down[layer]),
                     final_gain=final_norm if last else None)
        outs.append(h)
    return jnp.stack(outs, axis=0)
```

```python
import functools

import jax
import jax.numpy as jnp
from jax import lax
from jax.experimental import pallas as pl
from jax.experimental.pallas import tpu as pltpu

F32 = jnp.float32
BF16 = jnp.bfloat16

EPS = 1e-6
TINY = 1e-20
NEG_INF = -1e30

LANES = 128
VMEM_LIMIT = 56 * 1024 * 1024

HG_DK = 128
HG_SUB = 16
SSD_P = 64
SSD_N = 128
SSD_G = 4
SSD_CHUNK = 128
SSD_CONV = 4
FOX_HD = 64
CUM_PIECES = 3


def _params(*sem):
    return pltpu.CompilerParams(dimension_semantics=sem, vmem_limit_bytes=VMEM_LIMIT)


def _rms_scale(x):
    return lax.rsqrt(jnp.mean(x * x, axis=-1, keepdims=True) + EPS)


def _sigmoid(x):
    return 1.0 / (1.0 + jnp.exp(-x))


def _cumsum_rows(x, period):
    row = lax.broadcasted_iota(jnp.int32, x.shape, 0) % period
    shift = 1
    while shift < period:
        x = x + jnp.where(row >= shift, pltpu.roll(x, shift, 0), 0.0)
        shift *= 2
    return x


def _ffn_kernel(h_ref, g_ref, wg_ref, wu_ref, wd_ref, fg_ref, o_ref, u_ref, acc_ref, *, final_norm):
    j = pl.program_id(1)

    @pl.when(j == 0)
    def _():
        x = h_ref[...]
        u_ref[...] = (x * _rms_scale(x) * g_ref[...]).astype(BF16)
        acc_ref[...] = jnp.zeros_like(acc_ref)

    u = u_ref[...]
    gate = jnp.dot(u, wg_ref[...], preferred_element_type=F32)
    up = jnp.dot(u, wu_ref[...], preferred_element_type=F32)
    act = (gate * _sigmoid(gate) * up).astype(BF16)
    acc_ref[...] += jnp.dot(act, wd_ref[...], preferred_element_type=F32)

    @pl.when(j == pl.num_programs(1) - 1)
    def _():
        y = h_ref[...] + 0.5 * acc_ref[...]
        if final_norm:
            y = y * _rms_scale(y) * fg_ref[...]
        o_ref[...] = y


def _ffn(h, gain, wg, wu, wd, final_gain=None, *, tm=1024, tf=256):
    s, d = h.shape
    ff = wg.shape[1]
    assert s % tm == 0 and ff % tf == 0
    final_norm = final_gain is not None
    fg = final_gain if final_norm else gain
    return pl.pallas_call(
        functools.partial(_ffn_kernel, final_norm=final_norm),
        grid=(s // tm, ff // tf),
        in_specs=[
            pl.BlockSpec((tm, d), lambda i, j: (i, 0)),
            pl.BlockSpec((1, d), lambda i, j: (0, 0)),
            pl.BlockSpec((d, tf), lambda i, j: (0, j)),
            pl.BlockSpec((d, tf), lambda i, j: (0, j)),
            pl.BlockSpec((tf, d), lambda i, j: (j, 0)),
            pl.BlockSpec((1, d), lambda i, j: (0, 0)),
        ],
        out_specs=pl.BlockSpec((tm, d), lambda i, j: (i, 0)),
        out_shape=jax.ShapeDtypeStruct((s, d), F32),
        scratch_shapes=[pltpu.VMEM((tm, d), BF16), pltpu.VMEM((tm, d), F32)],
        compiler_params=_params("parallel", "arbitrary"),
        name="ffn",
    )(h, gain.reshape(1, d), wg, wu, wd, fg.reshape(1, d))


def _proj_kernel(h_ref, g_ref, w_ref, ws_ref, o_ref, os_ref, u_ref):
    j = pl.program_id(1)

    @pl.when(j == 0)
    def _():
        x = h_ref[...]
        u = (x * _rms_scale(x) * g_ref[...]).astype(BF16)
        u_ref[...] = u
        os_ref[...] = jnp.dot(u, ws_ref[...], preferred_element_type=F32)

    o_ref[...] = jnp.dot(u_ref[...], w_ref[...], preferred_element_type=F32)


def _norm_proj(h, gain, w_main, w_small, *, tm=1024, tn=1024):
    s, d = h.shape
    n = w_main.shape[1]
    assert s % tm == 0 and n % tn == 0
    return pl.pallas_call(
        _proj_kernel,
        grid=(s // tm, n // tn),
        in_specs=[
            pl.BlockSpec((tm, d), lambda i, j: (i, 0)),
            pl.BlockSpec((1, d), lambda i, j: (0, 0)),
            pl.BlockSpec((d, tn), lambda i, j: (0, j)),
            pl.BlockSpec((d, LANES), lambda i, j: (0, 0)),
        ],
        out_specs=[
            pl.BlockSpec((tm, tn), lambda i, j: (i, j)),
            pl.BlockSpec((tm, LANES), lambda i, j: (i, 0)),
        ],
        out_shape=[jax.ShapeDtypeStruct((s, n), F32), jax.ShapeDtypeStruct((s, LANES), F32)],
        scratch_shapes=[pltpu.VMEM((tm, d), BF16)],
        compiler_params=_params("parallel", "arbitrary"),
        name="norm_proj",
    )(h, gain.reshape(1, d), w_main, w_small)


def _out_proj_kernel(*refs, n_in):
    h_ref = refs[0]
    x_refs = refs[1:1 + n_in]
    w_refs = refs[1 + n_in:1 + 2 * n_in]
    o_ref = refs[1 + 2 * n_in]
    y = h_ref[...]
    for x_ref, w_ref in zip(x_refs, w_refs):
        y = y + jnp.dot(x_ref[...], w_ref[...], preferred_element_type=F32)
    o_ref[...] = y


def _out_proj(h, xs, ws, *, tm=1024):
    s, d = h.shape
    n_in = len(xs)
    in_specs = [pl.BlockSpec((tm, d), lambda i: (i, 0))]
    in_specs += [pl.BlockSpec((tm, x.shape[1]), lambda i: (i, 0)) for x in xs]
    in_specs += [pl.BlockSpec(w.shape, lambda i: (0, 0)) for w in ws]
    return pl.pallas_call(
        functools.partial(_out_proj_kernel, n_in=n_in),
        grid=(s // tm,),
        in_specs=in_specs,
        out_specs=pl.BlockSpec((tm, d), lambda i: (i, 0)),
        out_shape=jax.ShapeDtypeStruct((s, d), F32),
        compiler_params=_params("parallel"),
        name="out_proj",
    )(h, *xs, *ws)


def _hgrn_kernel(q_ref, f_ref, v_ref, g_ref, lbl_ref, gain_ref, o_ref,
                 state_ref, kv_ref, sprev_ref, *, layer_j, tb):
    c = HG_SUB
    nb = tb // c

    @pl.when(pl.program_id(1) == 0)
    def _():
        state_ref[...] = jnp.zeros_like(state_ref)

    logits = lbl_ref[...]
    e = jnp.exp(logits - jnp.max(logits, axis=0, keepdims=True))
    p = e / jnp.sum(e, axis=0, keepdims=True)
    lb = jnp.zeros((1, HG_DK), F32)
    for i in range(1, layer_j + 1):
        lb = lb + p[i:i + 1]

    fl = f_ref[...]
    f = lb + (1.0 - lb) * _sigmoid(fl)
    logf = jnp.log(jnp.maximum(f, TINY))
    k = (1.0 - lb) * _sigmoid(-fl)
    q = q_ref[...]
    v = v_ref[...]

    b = _cumsum_rows(logf, c)
    b3 = b.reshape(nb, c, HG_DK)
    q3 = q.reshape(nb, c, HG_DK)
    k3 = k.reshape(nb, c, HG_DK)
    v3 = v.reshape(nb, c, HG_DK)
    b_last = b3[:, c - 1:c, :]

    kd = (k3 * jnp.exp(b_last - b3)).astype(BF16)
    kv_ref[...] = lax.dot_general(v3.astype(BF16), kd, (((1,), (1,)), ((0,), (0,))),
                                  preferred_element_type=F32)
    dec = jnp.exp(b_last)

    s_cur = state_ref[...]
    for n in range(nb):
        sprev_ref[n] = s_cur.astype(BF16)
        s_cur = s_cur * dec[n] + kv_ref[n]
    state_ref[...] = s_cur

    qd = (q3 * jnp.exp(b3)).astype(BF16)
    o = lax.dot_general(qd, sprev_ref[...], (((2,), (2,)), ((0,), (0,))),
                        preferred_element_type=F32)

    row = lax.broadcasted_iota(jnp.int32, (nb, c, 1), 1)
    for s in range(c):
        live = row >= s
        diff = b3 - b3[:, s:s + 1, :]
        w = jnp.exp(jnp.where(live, diff, 0.0))
        col = jnp.sum(q3 * w * k3[:, s:s + 1, :], axis=-1, keepdims=True)
        o = o + jnp.where(live, col, 0.0) * v3[:, s:s + 1, :]

    o = o.reshape(tb, HG_DK)
    g = g_ref[...]
    o_ref[...] = (o * _rms_scale(o) * gain_ref[...] * (g * _sigmoid(g))).astype(o_ref.dtype)


def _hgrn(proj, lb_logits, gain, layer_j, *, tb=512):
    s = proj.shape[0]
    heads = gain.shape[0] // HG_DK
    n_even = lb_logits.shape[0]
    nb = tb // HG_SUB
    blk = lambda off: pl.BlockSpec((tb, HG_DK), lambda h, t, off=off: (t, off + h))
    return pl.pallas_call(
        functools.partial(_hgrn_kernel, layer_j=layer_j, tb=tb),
        grid=(heads, s // tb),
        in_specs=[
            blk(0), blk(heads), blk(2 * heads), blk(3 * heads),
            pl.BlockSpec((n_even, HG_DK), lambda h, t: (0, h)),
            pl.BlockSpec((1, HG_DK), lambda h, t: (0, h)),
        ],
        out_specs=pl.BlockSpec((tb, HG_DK), lambda h, t: (t, h)),
        out_shape=jax.ShapeDtypeStruct((s, heads * HG_DK), BF16),
        scratch_shapes=[
            pltpu.VMEM((HG_DK, HG_DK), F32),
            pltpu.VMEM((nb, HG_DK, HG_DK), F32),
            pltpu.VMEM((nb, HG_DK, HG_DK), BF16),
        ],
        compiler_params=_params("parallel", "arbitrary"),
        name="hgrn2",
    )(proj, proj, proj, proj, lb_logits, gain.reshape(1, -1))


def _split_bf16(x, parts):
    out = []
    for _ in range(parts):
        hi = x.astype(BF16)
        out.append(hi)
        x = x - hi.astype(F32)
    return out


def _ssd_kernel(z_ref, xs_ref, bc_ref, xsh_ref, bch_ref, dt_ref, cwx_ref, cwbc_ref, cbx_ref, cbbc_ref,
                dtb_ref, alog_ref, dskip_ref, gain_ref, expand_ref, o_ref, state_ref):
    L = SSD_CHUNK
    inner = z_ref.shape[1]
    heads = inner // SSD_P
    t = pl.program_id(0)

    @pl.when(t == 0)
    def _():
        state_ref[...] = jnp.zeros_like(state_ref)

    halo_on = (t > 0).astype(F32)

    def conv_silu(x_ref, halo_ref, w_ref, b_ref):
        x = x_ref[...]
        full = jnp.concatenate([halo_ref[...] * halo_on, x], axis=0)
        w = w_ref[...]
        acc = b_ref[...] + w[SSD_CONV - 1:SSD_CONV] * x
        for kk in range(SSD_CONV - 1):
            off = 8 - (SSD_CONV - 1) + kk
            acc = acc + w[kk:kk + 1] * full[off:off + L]
        return acc * _sigmoid(acc)

    xs = conv_silu(xs_ref, xsh_ref, cwx_ref, cbx_ref)
    bc = conv_silu(bc_ref, bch_ref, cwbc_ref, cbbc_ref)
    bm = bc[:, :SSD_G * SSD_N]
    cm = bc[:, SSD_G * SSD_N:]

    pre = dt_ref[...] + dtb_ref[...]
    dt = jnp.maximum(pre, 0.0) + jnp.log(1.0 + jnp.exp(-jnp.abs(pre)))
    a = dt * (-jnp.exp(alog_ref[...]))
    cs = _cumsum_rows(a, L)
    cs_t = cs.T

    expand = expand_ref[...]

    def spread(x, parts):
        y = jnp.zeros((x.shape[0], inner), F32)
        for piece in _split_bf16(x, parts):
            y = y + jnp.dot(piece, expand, preferred_element_type=F32)
        return y

    dt_e = spread(dt, 3)
    cs_e = spread(cs, 3)
    cs_last_e = cs_e[L - 1:L, :]
    xdt = xs * dt_e
    xdt_b = xdt.astype(BF16)
    w_state = (xdt * jnp.exp(cs_last_e - cs_e)).astype(BF16)
    grow = jnp.exp(cs_e)
    dec_e = jnp.exp(cs_last_e)

    tril = (lax.broadcasted_iota(jnp.int32, (L, L), 0) >= lax.broadcasted_iota(jnp.int32, (L, L), 1))
    lane = lax.broadcasted_iota(jnp.int32, (1, 2 * SSD_P), 1)
    rpg = heads // SSD_G
    gw = rpg * SSD_P
    y_parts = []
    for g in range(SSD_G):
        cg = cm[:, g * SSD_N:(g + 1) * SSD_N].astype(BF16)
        bg = bm[:, g * SSD_N:(g + 1) * SSD_N].astype(BF16)
        cb = lax.dot_general(cg, bg, (((1,), (1,)), ((), ())), preferred_element_type=F32)
        st = state_ref[:, g * gw:(g + 1) * gw]
        y_off = jnp.dot(cg, st.astype(BF16), preferred_element_type=F32)
        new = jnp.dot(bg.T, w_state[:, g * gw:(g + 1) * gw], preferred_element_type=F32)
        state_ref[:, g * gw:(g + 1) * gw] = st * dec_e[:, g * gw:(g + 1) * gw] + new
        for pr in range(rpg // 2):
            c0 = g * gw + pr * 2 * SSD_P
            x_pair = xdt_b[:, c0:c0 + 2 * SSD_P]
            y_pair = jnp.zeros((L, 2 * SSD_P), F32)
            for half in range(2):
                h = g * rpg + pr * 2 + half
                diff = cs[:, h:h + 1] - cs_t[h:h + 1, :]
                decay = jnp.where(tril, jnp.exp(jnp.where(tril, diff, 0.0)), 0.0)
                m = (cb * decay).astype(BF16)
                keep = (lane >= SSD_P) if half else (lane < SSD_P)
                x_half = jnp.where(keep, x_pair, jnp.zeros_like(x_pair))
                y_pair = y_pair + jnp.dot(m, x_half, preferred_element_type=F32)
            y_parts.append(y_pair + y_off[:, pr * 2 * SSD_P:(pr + 1) * 2 * SSD_P]
                           * grow[:, c0:c0 + 2 * SSD_P])
    y = jnp.concatenate(y_parts, axis=-1)
    y = y + xs * dskip_ref[...]
    z = z_ref[...]
    y = y * (z * _sigmoid(z))
    o_ref[...] = (y * _rms_scale(y) * gain_ref[...]).astype(o_ref.dtype)


def _ssd(proj, dt_raw, conv_w, conv_b, dt_bias, a_log, d_skip, gain, z_blk, xs_blk):
    s = proj.shape[0]
    inner = gain.shape[0]
    heads = inner // SSD_P
    L = SSD_CHUNK
    assert 2 * SSD_G * SSD_N == inner and heads <= LANES

    def pad_heads(x):
        return jnp.pad(x.astype(F32), (0, LANES - heads)).reshape(1, LANES)

    expand = (jnp.arange(LANES)[:, None] == (jnp.arange(inner)[None, :] // SSD_P)).astype(BF16)
    halo = lambda blk: pl.BlockSpec((8, inner), lambda t, blk=blk: (jnp.maximum(t * (L // 8) - 1, 0), blk))
    main = lambda blk: pl.BlockSpec((L, inner), lambda t, blk=blk: (t, blk))
    row = lambda width: pl.BlockSpec((1, width), lambda t: (0, 0))
    return pl.pallas_call(
        _ssd_kernel,
        grid=(s // L,),
        in_specs=[
            main(z_blk), main(xs_blk), main(xs_blk + 1), halo(xs_blk), halo(xs_blk + 1),
            pl.BlockSpec((L, LANES), lambda t: (t, 0)),
            pl.BlockSpec((SSD_CONV, inner), lambda t: (0, 0)),
            pl.BlockSpec((SSD_CONV, inner), lambda t: (0, 1)),
            pl.BlockSpec((1, inner), lambda t: (0, 0)),
            pl.BlockSpec((1, inner), lambda t: (0, 1)),
            row(LANES), row(LANES), row(inner), row(inner),
            pl.BlockSpec((LANES, inner), lambda t: (0, 0)),
        ],
        out_specs=pl.BlockSpec((L, inner), lambda t: (t, 0)),
        out_shape=jax.ShapeDtypeStruct((s, inner), BF16),
        scratch_shapes=[pltpu.VMEM((SSD_N, inner), F32)],
        compiler_params=_params("arbitrary"),
        name="ssd",
    )(proj, proj, proj, proj, proj, dt_raw, conv_w, conv_w, conv_b.reshape(1, -1), conv_b.reshape(1, -1),
      pad_heads(dt_bias), pad_heads(a_log), jnp.repeat(d_skip.astype(F32), SSD_P).reshape(1, inner),
      gain.reshape(1, inner), expand)


def _logf_cumsum_kernel(f_ref, b_ref, *refs):
    o_refs, carry_ref = refs[:-1], refs[-1]

    @pl.when(pl.program_id(0) == 0)
    def _():
        carry_ref[...] = jnp.zeros_like(carry_ref)

    x = f_ref[...] + b_ref[...]
    ls = jnp.minimum(x, 0.0) - jnp.log(1.0 + jnp.exp(-jnp.abs(x)))
    cum = _cumsum_rows(ls, x.shape[0]) + carry_ref[...]
    carry_ref[...] = cum[x.shape[0] - 1:, :]
    for o_ref, piece in zip(o_refs, _split_bf16(cum, len(o_refs))):
        o_ref[...] = piece


def _logf_cumsum(f_logit, b_f, *, tb=512):
    s = f_logit.shape[0]
    return pl.pallas_call(
        _logf_cumsum_kernel,
        grid=(s // tb,),
        in_specs=[pl.BlockSpec((tb, LANES), lambda t: (t, 0)), pl.BlockSpec((1, LANES), lambda t: (0, 0))],
        out_specs=[pl.BlockSpec((tb, LANES), lambda t: (t, 0))] * CUM_PIECES,
        out_shape=[jax.ShapeDtypeStruct((s, LANES), BF16)] * CUM_PIECES,
        scratch_shapes=[pltpu.VMEM((1, LANES), F32)],
        compiler_params=_params("arbitrary"),
        name="logf_cumsum",
    )(f_logit, b_f)


def _attn_kernel(q_ref, k_ref, v_ref, o_ref, m_ref, acc_ref, *, tq, tk):
    i = pl.program_id(1)
    m_ref[...] = jnp.full_like(m_ref, NEG_INF)
    acc_ref[...] = jnp.zeros_like(acc_ref)

    def step(j, row0, masked):
        rows = pl.ds(row0, tq - row0)
        n = tq - row0
        start = pl.multiple_of(j * tk, tk)
        k = k_ref[0, pl.ds(start, tk), :]
        v = v_ref[0, pl.ds(start, tk), :]
        s = lax.dot_general(q_ref[0, rows, :], k, (((1,), (1,)), ((), ())),
                            preferred_element_type=F32)
        if masked:
            qpos = lax.broadcasted_iota(jnp.int32, (n, tk), 0)
            kpos = lax.broadcasted_iota(jnp.int32, (n, tk), 1)
            s = jnp.where(kpos <= qpos, s, NEG_INF)
        m_old = m_ref[rows, :]
        m_new = jnp.maximum(m_old, jnp.max(s, axis=-1, keepdims=True))
        p = jnp.exp(s - pltpu.repeat(m_new, tk // LANES, axis=1)).astype(BF16)
        acc_ref[rows, :] = (jnp.exp(m_old - m_new) * acc_ref[rows, :]
                            + jnp.dot(p, v, preferred_element_type=F32))
        m_ref[rows, :] = m_new

    def body(j, carry):
        step(j, 0, False)
        return carry

    past = i * (tq // tk)
    lax.fori_loop(0, past, body, 0)
    for rel in range(tq // tk):
        step(past + rel, rel * tk, True)
    acc = acc_ref[...]
    o_ref[0] = (acc / acc[:, FOX_HD:FOX_HD + 1]).astype(o_ref.dtype)


def _attention(q_aug, k_aug, v_aug, *, tq=2048, tk=512):
    heads, s, w = q_aug.shape
    assert w == LANES and s % tq == 0 and tq % tk == 0
    return pl.pallas_call(
        functools.partial(_attn_kernel, tq=tq, tk=tk),
        grid=(heads, s // tq),
        in_specs=[
            pl.BlockSpec((1, tq, w), lambda h, i: (h, i, 0)),
            pl.BlockSpec((1, s, w), lambda h, i: (h, 0, 0)),
            pl.BlockSpec((1, s, w), lambda h, i: (h, 0, 0)),
        ],
        out_specs=pl.BlockSpec((1, tq, w), lambda h, i: (h, i, 0)),
        out_shape=jax.ShapeDtypeStruct((heads, s, w), BF16),
        scratch_shapes=[pltpu.VMEM((tq, LANES), F32), pltpu.VMEM((tq, w), F32)],
        compiler_params=_params("parallel", "arbitrary"),
        name="fox_attention",
    )(q_aug, k_aug, v_aug)


def _fox_operands(proj, cum_pieces, heads):
    s = proj.shape[0]
    width = heads * FOX_HD

    def to_heads(x):
        return x.reshape(s, heads, FOX_HD).transpose(1, 0, 2)

    q = (to_heads(proj[:, :width]) * (FOX_HD ** -0.5)).astype(BF16)
    k = to_heads(proj[:, width:2 * width]).astype(BF16)
    v = to_heads(proj[:, 2 * width:3 * width]).astype(BF16)
    cpos = jnp.stack([c[:, :heads].T for c in cum_pieces], axis=-1)
    ones3 = jnp.ones((heads, s, CUM_PIECES), BF16)
    zq = jnp.zeros((heads, s, LANES - FOX_HD - 2 * CUM_PIECES), BF16)
    q_aug = jnp.concatenate([q, cpos, ones3, zq], axis=-1)
    k_aug = jnp.concatenate([k, ones3, -cpos, zq], axis=-1)
    v_aug = jnp.concatenate([v, jnp.ones((heads, s, 1), BF16),
                             jnp.zeros((heads, s, LANES - FOX_HD - 1), BF16)], axis=-1)
    return q_aug, k_aug, v_aug


def _pad_cols(w, width):
    return jnp.pad(w, ((0, 0), (0, width - w.shape[1])))


def kernel(x, ffn1_norm, ffn1_w_gate, ffn1_w_up, ffn1_w_down, mix_norm, hyb_w_in, hgrn_lb_logits,
           hgrn_out_gain, ssd_conv_w, ssd_conv_b, ssd_dt_bias, ssd_A_log, ssd_D, ssd_out_gain, hyb_w_out,
           fox_w_in, fox_b_f, fox_w_out, ffn2_norm, ffn2_w_gate, ffn2_w_up, ffn2_w_down, final_norm):
    bsz, s, d = x.shape
    depth = ffn1_norm.shape[0]
    hg_width = hgrn_out_gain.shape[1]
    inner = ssd_out_gain.shape[1]
    ssd_heads = ssd_dt_bias.shape[1]
    fox_heads = fox_b_f.shape[1]
    fox_width = fox_heads * FOX_HD
    hyb_main = 4 * hg_width + inner + inner + 2 * SSD_G * SSD_N
    assert hg_width == inner == d and hyb_w_in.shape[2] == hyb_main + ssd_heads

    bf = lambda w: w.astype(BF16)
    outs = []
    for b in range(bsz):
        h = x[b]
        for layer in range(depth):
            j = layer // 2
            h = _ffn(h, ffn1_norm[layer], bf(ffn1_w_gate[layer]), bf(ffn1_w_up[layer]), bf(ffn1_w_down[layer]))
            if layer % 2 == 0:
                w_in = bf(hyb_w_in[j])
                proj, dt_raw = _norm_proj(h, mix_norm[layer], w_in[:, :hyb_main],
                                          _pad_cols(w_in[:, hyb_main:], LANES))
                o_a = _hgrn(proj, hgrn_lb_logits, hgrn_out_gain[j], j)
                o_b = _ssd(proj, dt_raw, ssd_conv_w[j], ssd_conv_b[j], ssd_dt_bias[j], ssd_A_log[j],
                           ssd_D[j], ssd_out_gain[j], z_blk=4 * hg_width // inner,
                           xs_blk=4 * hg_width // inner + 1)
                w_out = bf(hyb_w_out[j])
                h = _out_proj(h, [o_a, o_b], [w_out[:hg_width], w_out[hg_width:]])
            else:
                w_in = bf(fox_w_in[j])
                proj, f_logit = _norm_proj(h, mix_norm[layer], w_in[:, :3 * fox_width],
                                           _pad_cols(w_in[:, 3 * fox_width:], LANES))
                cum = _logf_cumsum(f_logit, jnp.pad(fox_b_f[j], (0, LANES - fox_heads)).reshape(1, LANES))
                o = _attention(*_fox_operands(proj, cum, fox_heads))
                o = o[:, :, :FOX_HD].transpose(1, 0, 2).reshape(s, fox_width)
                h = _out_proj(h, [o], [bf(fox_w_out[j])])
            last = layer == depth - 1
            h = _ffn(h, ffn2_norm[layer], bf(ffn2_w_gate[layer]), bf(ffn2_w_up[layer]), bf(ffn2_w_down[layer]),
                     final_gain=final_norm if last else None)
        outs.append(h)
    return jnp.stack(outs, axis=0)
```

```python
import functools

import jax
import jax.numpy as jnp
from jax import lax
from jax.experimental import pallas as pl
from jax.experimental.pallas import tpu as pltpu

F32 = jnp.float32
BF16 = jnp.bfloat16

EPS = 1e-6
TINY = 1e-20
NEG_INF = -1e30

LANES = 128
VMEM_LIMIT = 56 * 1024 * 1024

HG_DK = 128
HG_SUB = 16
SSD_P = 64
SSD_N = 128
SSD_G = 4
SSD_CHUNK = 128
SSD_CONV = 4
FOX_HD = 64
CUM_PIECES = 3


def _params(*sem):
    return pltpu.CompilerParams(dimension_semantics=sem, vmem_limit_bytes=VMEM_LIMIT)


def _rms_scale(x):
    return lax.rsqrt(jnp.mean(x * x, axis=-1, keepdims=True) + EPS)


def _sigmoid(x):
    return 1.0 / (1.0 + jnp.exp(-x))


def _cumsum_rows(x, period):
    row = lax.broadcasted_iota(jnp.int32, x.shape, 0) % period
    shift = 1
    while shift < period:
        x = x + jnp.where(row >= shift, pltpu.roll(x, shift, 0), 0.0)
        shift *= 2
    return x


def _ffn_kernel(h_ref, g_ref, wg_ref, wu_ref, wd_ref, fg_ref, o_ref, u_ref, acc_ref, *, final_norm):
    j = pl.program_id(1)

    @pl.when(j == 0)
    def _():
        x = h_ref[...]
        u_ref[...] = (x * _rms_scale(x) * g_ref[...]).astype(BF16)
        acc_ref[...] = jnp.zeros_like(acc_ref)

    u = u_ref[...]
    gate = jnp.dot(u, wg_ref[...], preferred_element_type=F32)
    up = jnp.dot(u, wu_ref[...], preferred_element_type=F32)
    act = (gate * _sigmoid(gate) * up).astype(BF16)
    acc_ref[...] += jnp.dot(act, wd_ref[...], preferred_element_type=F32)

    @pl.when(j == pl.num_programs(1) - 1)
    def _():
        y = h_ref[...] + 0.5 * acc_ref[...]
        if final_norm:
            y = y * _rms_scale(y) * fg_ref[...]
        o_ref[...] = y


def _ffn(h, gain, wg, wu, wd, layer, final_gain=None, *, tm=1024, tf=256):
    s, d = h.shape
    ff = wg.shape[2]
    assert s % tm == 0 and ff % tf == 0
    final_norm = final_gain is not None
    fg = final_gain if final_norm else gain
    return pl.pallas_call(
        functools.partial(_ffn_kernel, final_norm=final_norm),
        grid=(s // tm, ff // tf),
        in_specs=[
            pl.BlockSpec((tm, d), lambda i, j: (i, 0)),
            pl.BlockSpec((1, d), lambda i, j: (0, 0)),
            pl.BlockSpec((None, d, tf), lambda i, j: (layer, 0, j)),
            pl.BlockSpec((None, d, tf), lambda i, j: (layer, 0, j)),
            pl.BlockSpec((None, tf, d), lambda i, j: (layer, j, 0)),
            pl.BlockSpec((1, d), lambda i, j: (0, 0)),
        ],
        out_specs=pl.BlockSpec((tm, d), lambda i, j: (i, 0)),
        out_shape=jax.ShapeDtypeStruct((s, d), F32),
        scratch_shapes=[pltpu.VMEM((tm, d), BF16), pltpu.VMEM((tm, d), F32)],
        compiler_params=_params("parallel", "arbitrary"),
        name="ffn",
    )(h, gain.reshape(1, d), wg, wu, wd, fg.reshape(1, d))


def _proj_kernel(h_ref, g_ref, w_ref, ws_ref, o_ref, os_ref, u_ref):
    j = pl.program_id(1)

    @pl.when(j == 0)
    def _():
        x = h_ref[...]
        u = (x * _rms_scale(x) * g_ref[...]).astype(BF16)
        u_ref[...] = u
        os_ref[...] = jnp.dot(u, ws_ref[...], preferred_element_type=F32)

    o_ref[...] = jnp.dot(u_ref[...], w_ref[...], preferred_element_type=F32).astype(o_ref.dtype)


def _norm_proj(h, gain, w_main, layer, n, w_small, out_dtype, *, tm=1024, tn=1024):
    s, d = h.shape
    assert s % tm == 0 and n % tn == 0 and n <= w_main.shape[2]
    return pl.pallas_call(
        _proj_kernel,
        grid=(s // tm, n // tn),
        in_specs=[
            pl.BlockSpec((tm, d), lambda i, j: (i, 0)),
            pl.BlockSpec((1, d), lambda i, j: (0, 0)),
            pl.BlockSpec((None, d, tn), lambda i, j: (layer, 0, j)),
            pl.BlockSpec((d, LANES), lambda i, j: (0, 0)),
        ],
        out_specs=[
            pl.BlockSpec((tm, tn), lambda i, j: (i, j)),
            pl.BlockSpec((tm, LANES), lambda i, j: (i, 0)),
        ],
        out_shape=[jax.ShapeDtypeStruct((s, n), out_dtype), jax.ShapeDtypeStruct((s, LANES), F32)],
        scratch_shapes=[pltpu.VMEM((tm, d), BF16)],
        compiler_params=_params("parallel", "arbitrary"),
        name="norm_proj",
    )(h, gain.reshape(1, d), w_main, w_small)


def _out_proj_kernel(*refs, n_in):
    h_ref = refs[0]
    x_refs = refs[1:1 + n_in]
    w_refs = refs[1 + n_in:1 + 2 * n_in]
    o_ref = refs[1 + 2 * n_in]
    y = h_ref[...]
    for x_ref, w_ref in zip(x_refs, w_refs):
        y = y + jnp.dot(x_ref[...], w_ref[...], preferred_element_type=F32)
    o_ref[...] = y


def _out_proj(h, xs, w, layer, *, tm=1024):
    s, d = h.shape
    n_in = len(xs)
    in_specs = [pl.BlockSpec((tm, d), lambda i: (i, 0))]
    in_specs += [pl.BlockSpec((tm, x.shape[1]), lambda i: (i, 0)) for x in xs]
    row = 0
    for x in xs:
        width = x.shape[1]
        assert row % width == 0
        in_specs.append(pl.BlockSpec((None, width, d), lambda i, blk=row // width: (layer, blk, 0)))
        row += width
    assert row == w.shape[1]
    return pl.pallas_call(
        functools.partial(_out_proj_kernel, n_in=n_in),
        grid=(s // tm,),
        in_specs=in_specs,
        out_specs=pl.BlockSpec((tm, d), lambda i: (i, 0)),
        out_shape=jax.ShapeDtypeStruct((s, d), F32),
        compiler_params=_params("parallel"),
        name="out_proj",
    )(h, *xs, *([w] * n_in))


def _hgrn_kernel(q_ref, f_ref, v_ref, g_ref, lbl_ref, gain_ref, o_ref,
                 state_ref, kv_ref, sprev_ref, *, layer_j, tb):
    c = HG_SUB
    nb = tb // c

    @pl.when(pl.program_id(1) == 0)
    def _():
        state_ref[...] = jnp.zeros_like(state_ref)

    logits = lbl_ref[...]
    e = jnp.exp(logits - jnp.max(logits, axis=0, keepdims=True))
    p = e / jnp.sum(e, axis=0, keepdims=True)
    lb = jnp.zeros((1, HG_DK), F32)
    for i in range(1, layer_j + 1):
        lb = lb + p[i:i + 1]

    fl = f_ref[...]
    f = lb + (1.0 - lb) * _sigmoid(fl)
    logf = jnp.log(jnp.maximum(f, TINY))
    k = (1.0 - lb) * _sigmoid(-fl)
    q = q_ref[...]
    v = v_ref[...]

    b = _cumsum_rows(logf, c)
    b3 = b.reshape(nb, c, HG_DK)
    q3 = q.reshape(nb, c, HG_DK)
    k3 = k.reshape(nb, c, HG_DK)
    v3 = v.reshape(nb, c, HG_DK)
    b_last = b3[:, c - 1:c, :]

    kd = (k3 * jnp.exp(b_last - b3)).astype(BF16)
    kv_ref[...] = lax.dot_general(v3.astype(BF16), kd, (((1,), (1,)), ((0,), (0,))),
                                  preferred_element_type=F32)
    dec = jnp.exp(b_last)

    s_cur = state_ref[...]
    for n in range(nb):
        sprev_ref[n] = s_cur.astype(BF16)
        s_cur = s_cur * dec[n] + kv_ref[n]
    state_ref[...] = s_cur

    qd = (q3 * jnp.exp(b3)).astype(BF16)
    o = lax.dot_general(qd, sprev_ref[...], (((2,), (2,)), ((0,), (0,))),
                        preferred_element_type=F32)

    row = lax.broadcasted_iota(jnp.int32, (nb, c, 1), 1)
    for s in range(c):
        live = row >= s
        diff = b3 - b3[:, s:s + 1, :]
        w = jnp.exp(jnp.where(live, diff, 0.0))
        col = jnp.sum(q3 * w * k3[:, s:s + 1, :], axis=-1, keepdims=True)
        o = o + jnp.where(live, col, 0.0) * v3[:, s:s + 1, :]

    o = o.reshape(tb, HG_DK)
    g = g_ref[...]
    o_ref[...] = (o * _rms_scale(o) * gain_ref[...] * (g * _sigmoid(g))).astype(o_ref.dtype)


def _hgrn(proj, lb_logits, gain, layer_j, *, tb=512):
    s = proj.shape[0]
    heads = gain.shape[0] // HG_DK
    n_even = lb_logits.shape[0]
    nb = tb // HG_SUB
    blk = lambda off: pl.BlockSpec((tb, HG_DK), lambda h, t, off=off: (t, off + h))
    return pl.pallas_call(
        functools.partial(_hgrn_kernel, layer_j=layer_j, tb=tb),
        grid=(heads, s // tb),
        in_specs=[
            blk(0), blk(heads), blk(2 * heads), blk(3 * heads),
            pl.BlockSpec((n_even, HG_DK), lambda h, t: (0, h)),
            pl.BlockSpec((1, HG_DK), lambda h, t: (0, h)),
        ],
        out_specs=pl.BlockSpec((tb, HG_DK), lambda h, t: (t, h)),
        out_shape=jax.ShapeDtypeStruct((s, heads * HG_DK), BF16),
        scratch_shapes=[
            pltpu.VMEM((HG_DK, HG_DK), F32),
            pltpu.VMEM((nb, HG_DK, HG_DK), F32),
            pltpu.VMEM((nb, HG_DK, HG_DK), BF16),
        ],
        compiler_params=_params("parallel", "arbitrary"),
        name="hgrn2",
    )(proj, proj, proj, proj, lb_logits, gain.reshape(1, -1))


def _split_bf16(x, parts):
    out = []
    for _ in range(parts):
        hi = x.astype(BF16)
        out.append(hi)
        x = x - hi.astype(F32)
    return out


def _ssd_kernel(z_ref, xs_ref, bc_ref, xsh_ref, bch_ref, dt_ref, cwx_ref, cwbc_ref, cbx_ref, cbbc_ref,
                dtb_ref, alog_ref, dskip_ref, gain_ref, expand_ref, o_ref, state_ref):
    L = SSD_CHUNK
    inner = z_ref.shape[1]
    heads = inner // SSD_P
    t = pl.program_id(0)

    @pl.when(t == 0)
    def _():
        state_ref[...] = jnp.zeros_like(state_ref)

    halo_on = (t > 0).astype(F32)

    def conv_silu(x_ref, halo_ref, w_ref, b_ref):
        x = x_ref[...]
        full = jnp.concatenate([halo_ref[...] * halo_on, x], axis=0)
        w = w_ref[...]
        acc = b_ref[...] + w[SSD_CONV - 1:SSD_CONV] * x
        for kk in range(SSD_CONV - 1):
            off = 8 - (SSD_CONV - 1) + kk
            acc = acc + w[kk:kk + 1] * full[off:off + L]
        return acc * _sigmoid(acc)

    xs = conv_silu(xs_ref, xsh_ref, cwx_ref, cbx_ref)
    bc = conv_silu(bc_ref, bch_ref, cwbc_ref, cbbc_ref)
    bm = bc[:, :SSD_G * SSD_N]
    cm = bc[:, SSD_G * SSD_N:]

    pre = dt_ref[...] + dtb_ref[...]
    dt = jnp.maximum(pre, 0.0) + jnp.log(1.0 + jnp.exp(-jnp.abs(pre)))
    a = dt * (-jnp.exp(alog_ref[...]))
    cs = _cumsum_rows(a, L)
    cs_t = cs.T

    expand = expand_ref[...]

    def spread(x, parts):
        y = jnp.zeros((x.shape[0], inner), F32)
        for piece in _split_bf16(x, parts):
            y = y + jnp.dot(piece, expand, preferred_element_type=F32)
        return y

    dt_e = spread(dt, 3)
    cs_e = spread(cs, 3)
    cs_last_e = cs_e[L - 1:L, :]
    xdt = xs * dt_e
    xdt_b = xdt.astype(BF16)
    w_state = (xdt * jnp.exp(cs_last_e - cs_e)).astype(BF16)
    grow = jnp.exp(cs_e)
    dec_e = jnp.exp(cs_last_e)

    tril = (lax.broadcasted_iota(jnp.int32, (L, L), 0) >= lax.broadcasted_iota(jnp.int32, (L, L), 1))
    lane = lax.broadcasted_iota(jnp.int32, (1, 2 * SSD_P), 1)
    rpg = heads // SSD_G
    gw = rpg * SSD_P
    y_parts = []
    for g in range(SSD_G):
        cg = cm[:, g * SSD_N:(g + 1) * SSD_N].astype(BF16)
        bg = bm[:, g * SSD_N:(g + 1) * SSD_N].astype(BF16)
        cb = lax.dot_general(cg, bg, (((1,), (1,)), ((), ())), preferred_element_type=F32)
        st = state_ref[:, g * gw:(g + 1) * gw]
        y_off = jnp.dot(cg, st.astype(BF16), preferred_element_type=F32)
        new = jnp.dot(bg.T, w_state[:, g * gw:(g + 1) * gw], preferred_element_type=F32)
        state_ref[:, g * gw:(g + 1) * gw] = st * dec_e[:, g * gw:(g + 1) * gw] + new
        for pr in range(rpg // 2):
            c0 = g * gw + pr * 2 * SSD_P
            x_pair = xdt_b[:, c0:c0 + 2 * SSD_P]
            y_pair = jnp.zeros((L, 2 * SSD_P), F32)
            for half in range(2):
                h = g * rpg + pr * 2 + half
                diff = cs[:, h:h + 1] - cs_t[h:h + 1, :]
                decay = jnp.where(tril, jnp.exp(jnp.where(tril, diff, 0.0)), 0.0)
                m = (cb * decay).astype(BF16)
                keep = (lane >= SSD_P) if half else (lane < SSD_P)
                x_half = jnp.where(keep, x_pair, jnp.zeros_like(x_pair))
                y_pair = y_pair + jnp.dot(m, x_half, preferred_element_type=F32)
            y_parts.append(y_pair + y_off[:, pr * 2 * SSD_P:(pr + 1) * 2 * SSD_P]
                           * grow[:, c0:c0 + 2 * SSD_P])
    y = jnp.concatenate(y_parts, axis=-1)
    y = y + xs * dskip_ref[...]
    z = z_ref[...]
    y = y * (z * _sigmoid(z))
    o_ref[...] = (y * _rms_scale(y) * gain_ref[...]).astype(o_ref.dtype)


def _ssd(proj, dt_raw, conv_w, conv_b, dt_bias, a_log, d_skip, gain, z_blk, xs_blk):
    s = proj.shape[0]
    inner = gain.shape[0]
    heads = inner // SSD_P
    L = SSD_CHUNK
    assert 2 * SSD_G * SSD_N == inner and heads <= LANES

    def pad_heads(x):
        return jnp.pad(x.astype(F32), (0, LANES - heads)).reshape(1, LANES)

    expand = (jnp.arange(LANES)[:, None] == (jnp.arange(inner)[None, :] // SSD_P)).astype(BF16)
    halo = lambda blk: pl.BlockSpec((8, inner), lambda t, blk=blk: (jnp.maximum(t * (L // 8) - 1, 0), blk))
    main = lambda blk: pl.BlockSpec((L, inner), lambda t, blk=blk: (t, blk))
    row = lambda width: pl.BlockSpec((1, width), lambda t: (0, 0))
    return pl.pallas_call(
        _ssd_kernel,
        grid=(s // L,),
        in_specs=[
            main(z_blk), main(xs_blk), main(xs_blk + 1), halo(xs_blk), halo(xs_blk + 1),
            pl.BlockSpec((L, LANES), lambda t: (t, 0)),
            pl.BlockSpec((SSD_CONV, inner), lambda t: (0, 0)),
            pl.BlockSpec((SSD_CONV, inner), lambda t: (0, 1)),
            pl.BlockSpec((1, inner), lambda t: (0, 0)),
            pl.BlockSpec((1, inner), lambda t: (0, 1)),
            row(LANES), row(LANES), row(inner), row(inner),
            pl.BlockSpec((LANES, inner), lambda t: (0, 0)),
        ],
        out_specs=pl.BlockSpec((L, inner), lambda t: (t, 0)),
        out_shape=jax.ShapeDtypeStruct((s, inner), BF16),
        scratch_shapes=[pltpu.VMEM((SSD_N, inner), F32)],
        compiler_params=_params("arbitrary"),
        name="ssd",
    )(proj, proj, proj, proj, proj, dt_raw, conv_w, conv_w, conv_b.reshape(1, -1), conv_b.reshape(1, -1),
      pad_heads(dt_bias), pad_heads(a_log), jnp.repeat(d_skip.astype(F32), SSD_P).reshape(1, inner),
      gain.reshape(1, inner), expand)


def _logf_cumsum_kernel(f_ref, b_ref, o_ref, carry_ref, *, heads):
    @pl.when(pl.program_id(0) == 0)
    def _():
        carry_ref[...] = jnp.zeros_like(carry_ref)

    x = f_ref[...] + b_ref[...]
    ls = jnp.minimum(x, 0.0) - jnp.log(1.0 + jnp.exp(-jnp.abs(x)))
    cum = _cumsum_rows(ls, x.shape[0]) + carry_ref[...]
    carry_ref[...] = cum[x.shape[0] - 1:, :]
    lane = lax.broadcasted_iota(jnp.int32, cum.shape, 1)
    out = jnp.zeros_like(cum)
    for p, piece in enumerate(_split_bf16(cum, CUM_PIECES)):
        piece = jnp.where(lane < heads, piece.astype(F32), 0.0)
        out = out + (pltpu.roll(piece, p * heads, 1) if p else piece)
    o_ref[...] = out.astype(BF16)


def _logf_cumsum(f_logit, b_f, heads, *, tb=512):
    s = f_logit.shape[0]
    assert CUM_PIECES * heads <= LANES
    return pl.pallas_call(
        functools.partial(_logf_cumsum_kernel, heads=heads),
        grid=(s // tb,),
        in_specs=[pl.BlockSpec((tb, LANES), lambda t: (t, 0)), pl.BlockSpec((1, LANES), lambda t: (0, 0))],
        out_specs=pl.BlockSpec((tb, LANES), lambda t: (t, 0)),
        out_shape=jax.ShapeDtypeStruct((s, LANES), BF16),
        scratch_shapes=[pltpu.VMEM((1, LANES), F32)],
        compiler_params=_params("arbitrary"),
        name="logf_cumsum",
    )(f_logit, b_f)


def _attn_kernel(q_ref, k_ref, v_ref, cq_ref, ck_ref, selq_ref, selk_ref, o_ref,
                 kaug_ref, vaug_ref, m_ref, acc_ref, *, tq, tk):
    i = pl.program_id(1)
    s_len = k_ref.shape[0]
    lane = lax.broadcasted_iota(jnp.int32, (1, LANES), 1)
    ones_k = ((lane >= FOX_HD) & (lane < FOX_HD + CUM_PIECES)).astype(F32)
    ones_q = ((lane >= FOX_HD + CUM_PIECES) & (lane < FOX_HD + 2 * CUM_PIECES)).astype(F32)
    ones_v = (lane == FOX_HD).astype(F32)

    @pl.when(i == 0)
    def _():
        def fill(t, carry):
            rows = pl.ds(pl.multiple_of(t * tq, tq), tq)
            bias = jnp.dot(ck_ref[rows, :], selk_ref[0], preferred_element_type=F32)
            kaug_ref[rows, :] = (k_ref[rows, :].astype(F32) + bias + ones_k).astype(BF16)
            vaug_ref[rows, :] = (v_ref[rows, :].astype(F32) + ones_v).astype(BF16)
            return carry
        lax.fori_loop(0, s_len // tq, fill, 0)

    m_ref[...] = jnp.full_like(m_ref, NEG_INF)
    acc_ref[...] = jnp.zeros_like(acc_ref)
    q_bias = jnp.dot(cq_ref[...], selq_ref[0], preferred_element_type=F32)
    q_aug = (q_ref[...].astype(F32) + q_bias + ones_q).astype(BF16)

    def step(j, row0, masked):
        n = tq - row0
        start = pl.multiple_of(j * tk, tk)
        k = kaug_ref[pl.ds(start, tk), :]
        v = vaug_ref[pl.ds(start, tk), :]
        s = lax.dot_general(q_aug[row0:, :], k, (((1,), (1,)), ((), ())),
                            preferred_element_type=F32)
        if masked:
            qpos = lax.broadcasted_iota(jnp.int32, (n, tk), 0)
            kpos = lax.broadcasted_iota(jnp.int32, (n, tk), 1)
            s = jnp.where(kpos <= qpos, s, NEG_INF)
        rows = pl.ds(row0, n)
        m_old = m_ref[rows, :]
        m_new = jnp.maximum(m_old, jnp.max(s, axis=-1, keepdims=True))
        p = jnp.exp(s - jnp.concatenate([m_new] * (tk // LANES), axis=1)).astype(BF16)
        acc_ref[rows, :] = (jnp.exp(m_old - m_new) * acc_ref[rows, :]
                            + jnp.dot(p, v, preferred_element_type=F32))
        m_ref[rows, :] = m_new

    def body(j, carry):
        step(j, 0, False)
        return carry

    past = i * (tq // tk)
    lax.fori_loop(0, past, body, 0)
    for rel in range(tq // tk):
        step(past + rel, rel * tk, True)
    acc = acc_ref[...]
    o_ref[...] = (acc / acc[:, FOX_HD:FOX_HD + 1]).astype(o_ref.dtype)


def _attention(qkv, cum, heads, *, tq=2048, tk=512):
    s = qkv.shape[0]
    assert s % tq == 0 and tq % tk == 0
    eye = jnp.eye(LANES, dtype=F32)
    src = jnp.arange(CUM_PIECES)[None, :] * heads + jnp.arange(heads)[:, None]
    pick = eye[src]

    def selector(dst0, sign):
        dst = eye[dst0 + jnp.arange(CUM_PIECES)]
        return (sign * jnp.einsum("hpa,pb->hab", pick, dst)).astype(BF16)

    selq = selector(FOX_HD, 1.0)
    selk = selector(FOX_HD + CUM_PIECES, -1.0)
    return pl.pallas_call(
        functools.partial(_attn_kernel, tq=tq, tk=tk),
        grid=(heads, s // tq),
        in_specs=[
            pl.BlockSpec((tq, LANES), lambda h, i: (i, h)),
            pl.BlockSpec((s, LANES), lambda h, i: (0, heads + h)),
            pl.BlockSpec((s, LANES), lambda h, i: (0, 2 * heads + h)),
            pl.BlockSpec((tq, LANES), lambda h, i: (i, 0)),
            pl.BlockSpec((s, LANES), lambda h, i: (0, 0)),
            pl.BlockSpec((1, LANES, LANES), lambda h, i: (h, 0, 0)),
            pl.BlockSpec((1, LANES, LANES), lambda h, i: (h, 0, 0)),
        ],
        out_specs=pl.BlockSpec((tq, LANES), lambda h, i: (i, h)),
        out_shape=jax.ShapeDtypeStruct((s, heads * LANES), BF16),
        scratch_shapes=[pltpu.VMEM((s, LANES), BF16), pltpu.VMEM((s, LANES), BF16),
                        pltpu.VMEM((tq, LANES), F32), pltpu.VMEM((tq, LANES), F32)],
        compiler_params=_params("arbitrary", "arbitrary"),
        name="fox_attention",
    )(qkv, qkv, qkv, cum, cum, selq, selk)


def _pad_heads_cols(w, heads, scale=1.0):
    d = w.shape[0]
    w = (w * scale).reshape(d, heads, FOX_HD)
    return jnp.pad(w, ((0, 0), (0, 0), (0, LANES - FOX_HD))).reshape(d, heads * LANES)


def _pad_cols(w, width):
    return jnp.pad(w, ((0, 0), (0, width - w.shape[1])))


def kernel(x, ffn1_norm, ffn1_w_gate, ffn1_w_up, ffn1_w_down, mix_norm, hyb_w_in, hgrn_lb_logits,
           hgrn_out_gain, ssd_conv_w, ssd_conv_b, ssd_dt_bias, ssd_A_log, ssd_D, ssd_out_gain, hyb_w_out,
           fox_w_in, fox_b_f, fox_w_out, ffn2_norm, ffn2_w_gate, ffn2_w_up, ffn2_w_down, final_norm):
    bsz, s, d = x.shape
    depth = ffn1_norm.shape[0]
    hg_width = hgrn_out_gain.shape[1]
    inner = ssd_out_gain.shape[1]
    ssd_heads = ssd_dt_bias.shape[1]
    fox_heads = fox_b_f.shape[1]
    fox_width = fox_heads * FOX_HD
    hyb_main = 4 * hg_width + inner + inner + 2 * SSD_G * SSD_N
    assert hg_width == inner == d and hyb_w_in.shape[2] == hyb_main + ssd_heads

    bf = lambda w: w.astype(BF16)
    ffn1 = (bf(ffn1_w_gate), bf(ffn1_w_up), bf(ffn1_w_down))
    ffn2 = (bf(ffn2_w_gate), bf(ffn2_w_up), bf(ffn2_w_down))
    hyb_in, hyb_out = bf(hyb_w_in), bf(hyb_w_out)
    fox_in, fox_small, fox_out = [], [], []
    for j in range(fox_w_in.shape[0]):
        wq, wk, wv, wf = jnp.split(fox_w_in[j], (fox_width, 2 * fox_width, 3 * fox_width), axis=1)
        fox_in.append(bf(jnp.concatenate([_pad_heads_cols(wq, fox_heads, FOX_HD ** -0.5),
                                          _pad_heads_cols(wk, fox_heads),
                                          _pad_heads_cols(wv, fox_heads)], axis=1))[None])
        fox_small.append(_pad_cols(bf(wf), LANES))
        fox_out.append(bf(_pad_heads_cols(fox_w_out[j].T, fox_heads).T)[None])

    outs = []
    for b in range(bsz):
        h = x[b]
        for layer in range(depth):
            j = layer // 2
            h = _ffn(h, ffn1_norm[layer], *ffn1, layer)
            if layer % 2 == 0:
                proj, dt_raw = _norm_proj(h, mix_norm[layer], hyb_in, j, hyb_main,
                                          _pad_cols(hyb_in[j, :, hyb_main:], LANES), F32)
                o_a = _hgrn(proj, hgrn_lb_logits, hgrn_out_gain[j], j)
                o_b = _ssd(proj, dt_raw, ssd_conv_w[j], ssd_conv_b[j], ssd_dt_bias[j], ssd_A_log[j],
                           ssd_D[j], ssd_out_gain[j], z_blk=4 * hg_width // inner,
                           xs_blk=4 * hg_width // inner + 1)
                h = _out_proj(h, [o_a, o_b], hyb_out, j)
            else:
                qkv, f_logit = _norm_proj(h, mix_norm[layer], fox_in[j], 0, 3 * fox_heads * LANES,
                                          fox_small[j], BF16)
                cum = _logf_cumsum(f_logit, jnp.pad(fox_b_f[j], (0, LANES - fox_heads)).reshape(1, LANES),
                                   fox_heads)
                o = _attention(qkv, cum, fox_heads)
                h = _out_proj(h, [o], fox_out[j], 0)
            last = layer == depth - 1
            h = _ffn(h, ffn2_norm[layer], *ffn2, layer, final_gain=final_norm if last else None)
        outs.append(h)
    return jnp.stack(outs, axis=0)
```

```python
import functools

import jax
import jax.numpy as jnp
from jax import lax
from jax.experimental import pallas as pl
from jax.experimental.pallas import tpu as pltpu

F32 = jnp.float32
BF16 = jnp.bfloat16

EPS = 1e-6
TINY = 1e-20
NEG_INF = -1e30

LANES = 128
VMEM_LIMIT = 56 * 1024 * 1024

HG_DK = 128
HG_SUB = 16
SSD_P = 64
SSD_N = 128
SSD_G = 4
SSD_CHUNK = 128
SSD_CONV = 4
FOX_HD = 64
CUM_PIECES = 3


def _params(*sem):
    return pltpu.CompilerParams(dimension_semantics=sem, vmem_limit_bytes=VMEM_LIMIT)


def _rms_scale(x):
    return lax.rsqrt(jnp.mean(x * x, axis=-1, keepdims=True) + EPS)


def _sigmoid(x):
    return 1.0 / (1.0 + jnp.exp(-x))


def _cumsum_rows(x, period):
    row = lax.broadcasted_iota(jnp.int32, x.shape, 0) % period
    shift = 1
    while shift < period:
        x = x + jnp.where(row >= shift, pltpu.roll(x, shift, 0), 0.0)
        shift *= 2
    return x


def _ffn_kernel(h_ref, g_ref, wg_ref, wu_ref, wd_ref, fg_ref, o_ref, u_ref, acc_ref, *, tf, final_norm):
    x = h_ref[...]
    u_ref[...] = (x * _rms_scale(x) * g_ref[...]).astype(BF16)
    ff = wg_ref.shape[1]
    for c in range(ff // tf):
        u = u_ref[...]
        cols = slice(c * tf, (c + 1) * tf)
        gate = jnp.dot(u, wg_ref[:, cols], preferred_element_type=F32)
        up = jnp.dot(u, wu_ref[:, cols], preferred_element_type=F32)
        act = (gate * _sigmoid(gate) * up).astype(BF16)
        part = jnp.dot(act, wd_ref[cols, :], preferred_element_type=F32)
        if c == 0:
            acc_ref[...] = part
        else:
            acc_ref[...] += part
    y = h_ref[...] + 0.5 * acc_ref[...]
    if final_norm:
        y = y * _rms_scale(y) * fg_ref[...]
    o_ref[...] = y


def _ffn(h, gain, wg, wu, wd, layer, final_gain=None, *, tm=1024, tf=256):
    s, d = h.shape
    ff = wg.shape[2]
    assert s % tm == 0 and ff % tf == 0
    final_norm = final_gain is not None
    fg = final_gain if final_norm else gain
    resident = lambda shape: pl.BlockSpec((None,) + shape, lambda i: (layer, 0, 0),
                                          pipeline_mode=pl.Buffered(1))
    return pl.pallas_call(
        functools.partial(_ffn_kernel, tf=tf, final_norm=final_norm),
        grid=(s // tm,),
        in_specs=[
            pl.BlockSpec((tm, d), lambda i: (i, 0)),
            pl.BlockSpec((1, d), lambda i: (0, 0)),
            resident((d, ff)), resident((d, ff)), resident((ff, d)),
            pl.BlockSpec((1, d), lambda i: (0, 0)),
        ],
        out_specs=pl.BlockSpec((tm, d), lambda i: (i, 0)),
        out_shape=jax.ShapeDtypeStruct((s, d), F32),
        scratch_shapes=[pltpu.VMEM((tm, d), BF16), pltpu.VMEM((tm, d), F32)],
        compiler_params=_params("parallel"),
        name="ffn",
    )(h, gain.reshape(1, d), wg, wu, wd, fg.reshape(1, d))


def _proj_kernel(h_ref, g_ref, w_ref, ws_ref, o_ref, os_ref, u_ref):
    j = pl.program_id(1)

    @pl.when(j == 0)
    def _():
        x = h_ref[...]
        u = (x * _rms_scale(x) * g_ref[...]).astype(BF16)
        u_ref[...] = u
        os_ref[...] = jnp.dot(u, ws_ref[...], preferred_element_type=F32)

    o_ref[...] = jnp.dot(u_ref[...], w_ref[...], preferred_element_type=F32).astype(o_ref.dtype)


def _norm_proj(h, gain, w_main, layer, n, w_small, out_dtype, *, tm=1024, tn=1024):
    s, d = h.shape
    assert s % tm == 0 and n % tn == 0 and n <= w_main.shape[2]
    return pl.pallas_call(
        _proj_kernel,
        grid=(s // tm, n // tn),
        in_specs=[
            pl.BlockSpec((tm, d), lambda i, j: (i, 0)),
            pl.BlockSpec((1, d), lambda i, j: (0, 0)),
            pl.BlockSpec((None, d, tn), lambda i, j: (layer, 0, j)),
            pl.BlockSpec((d, LANES), lambda i, j: (0, 0)),
        ],
        out_specs=[
            pl.BlockSpec((tm, tn), lambda i, j: (i, j)),
            pl.BlockSpec((tm, LANES), lambda i, j: (i, 0)),
        ],
        out_shape=[jax.ShapeDtypeStruct((s, n), out_dtype), jax.ShapeDtypeStruct((s, LANES), F32)],
        scratch_shapes=[pltpu.VMEM((tm, d), BF16)],
        compiler_params=_params("parallel", "arbitrary"),
        name="norm_proj",
    )(h, gain.reshape(1, d), w_main, w_small)


def _out_proj_kernel(*refs, n_in):
    h_ref = refs[0]
    x_refs = refs[1:1 + n_in]
    w_refs = refs[1 + n_in:1 + 2 * n_in]
    o_ref = refs[1 + 2 * n_in]
    y = h_ref[...]
    for x_ref, w_ref in zip(x_refs, w_refs):
        y = y + jnp.dot(x_ref[...], w_ref[...], preferred_element_type=F32)
    o_ref[...] = y


def _out_proj(h, xs, w, layer, *, tm=1024):
    s, d = h.shape
    n_in = len(xs)
    in_specs = [pl.BlockSpec((tm, d), lambda i: (i, 0))]
    in_specs += [pl.BlockSpec((tm, x.shape[1]), lambda i: (i, 0)) for x in xs]
    row = 0
    for x in xs:
        width = x.shape[1]
        assert row % width == 0
        in_specs.append(pl.BlockSpec((None, width, d), lambda i, blk=row // width: (layer, blk, 0)))
        row += width
    assert row == w.shape[1]
    return pl.pallas_call(
        functools.partial(_out_proj_kernel, n_in=n_in),
        grid=(s // tm,),
        in_specs=in_specs,
        out_specs=pl.BlockSpec((tm, d), lambda i: (i, 0)),
        out_shape=jax.ShapeDtypeStruct((s, d), F32),
        compiler_params=_params("parallel"),
        name="out_proj",
    )(h, *xs, *([w] * n_in))


def _hgrn_kernel(q_ref, f_ref, v_ref, g_ref, lbl_ref, gain_ref, o_ref,
                 state_ref, kv_ref, sprev_ref, *, layer_j, tb):
    c = HG_SUB
    nb = tb // c

    @pl.when(pl.program_id(1) == 0)
    def _():
        state_ref[...] = jnp.zeros_like(state_ref)

    logits = lbl_ref[...]
    e = jnp.exp(logits - jnp.max(logits, axis=0, keepdims=True))
    p = e / jnp.sum(e, axis=0, keepdims=True)
    lb = jnp.zeros((1, HG_DK), F32)
    for i in range(1, layer_j + 1):
        lb = lb + p[i:i + 1]

    fl = f_ref[...]
    f = lb + (1.0 - lb) * _sigmoid(fl)
    logf = jnp.log(jnp.maximum(f, TINY))
    k = (1.0 - lb) * _sigmoid(-fl)
    q = q_ref[...]
    v = v_ref[...]

    b = _cumsum_rows(logf, c)
    b3 = b.reshape(nb, c, HG_DK)
    q3 = q.reshape(nb, c, HG_DK)
    k3 = k.reshape(nb, c, HG_DK)
    v3 = v.reshape(nb, c, HG_DK)
    b_last = b3[:, c - 1:c, :]

    kd = (k3 * jnp.exp(b_last - b3)).astype(BF16)
    kv_ref[...] = lax.dot_general(v3.astype(BF16), kd, (((1,), (1,)), ((0,), (0,))),
                                  preferred_element_type=F32)
    dec = jnp.exp(b_last)

    s_cur = state_ref[...]
    for n in range(nb):
        sprev_ref[n] = s_cur.astype(BF16)
        s_cur = s_cur * dec[n] + kv_ref[n]
    state_ref[...] = s_cur

    qd = (q3 * jnp.exp(b3)).astype(BF16)
    o = lax.dot_general(qd, sprev_ref[...], (((2,), (2,)), ((0,), (0,))),
                        preferred_element_type=F32)

    row = lax.broadcasted_iota(jnp.int32, (nb, c, 1), 1)
    for s in range(c):
        live = row >= s
        diff = b3 - b3[:, s:s + 1, :]
        w = jnp.exp(jnp.where(live, diff, 0.0))
        col = jnp.sum(q3 * w * k3[:, s:s + 1, :], axis=-1, keepdims=True)
        o = o + jnp.where(live, col, 0.0) * v3[:, s:s + 1, :]

    o = o.reshape(tb, HG_DK)
    g = g_ref[...]
    o_ref[...] = (o * _rms_scale(o) * gain_ref[...] * (g * _sigmoid(g))).astype(o_ref.dtype)


def _hgrn(proj, lb_logits, gain, layer_j, *, tb=512):
    s = proj.shape[0]
    heads = gain.shape[0] // HG_DK
    n_even = lb_logits.shape[0]
    nb = tb // HG_SUB
    blk = lambda off: pl.BlockSpec((tb, HG_DK), lambda h, t, off=off: (t, off + h))
    return pl.pallas_call(
        functools.partial(_hgrn_kernel, layer_j=layer_j, tb=tb),
        grid=(heads, s // tb),
        in_specs=[
            blk(0), blk(heads), blk(2 * heads), blk(3 * heads),
            pl.BlockSpec((n_even, HG_DK), lambda h, t: (0, h)),
            pl.BlockSpec((1, HG_DK), lambda h, t: (0, h)),
        ],
        out_specs=pl.BlockSpec((tb, HG_DK), lambda h, t: (t, h)),
        out_shape=jax.ShapeDtypeStruct((s, heads * HG_DK), BF16),
        scratch_shapes=[
            pltpu.VMEM((HG_DK, HG_DK), F32),
            pltpu.VMEM((nb, HG_DK, HG_DK), F32),
            pltpu.VMEM((nb, HG_DK, HG_DK), BF16),
        ],
        compiler_params=_params("parallel", "arbitrary"),
        name="hgrn2",
    )(proj, proj, proj, proj, lb_logits, gain.reshape(1, -1))


def _split_bf16(x, parts):
    out = []
    for _ in range(parts):
        hi = x.astype(BF16)
        out.append(hi)
        x = x - hi.astype(F32)
    return out


def _ssd_kernel(z_ref, xs_ref, bc_ref, xsh_ref, bch_ref, dt_ref, cwx_ref, cwbc_ref, cbx_ref, cbbc_ref,
                dtb_ref, alog_ref, dskip_ref, gain_ref, expand_ref, o_ref, state_ref):
    L = SSD_CHUNK
    inner = z_ref.shape[1]
    heads = inner // SSD_P
    t = pl.program_id(0)

    @pl.when(t == 0)
    def _():
        state_ref[...] = jnp.zeros_like(state_ref)

    halo_on = (t > 0).astype(F32)

    def conv_silu(x_ref, halo_ref, w_ref, b_ref):
        x = x_ref[...]
        full = jnp.concatenate([halo_ref[...] * halo_on, x], axis=0)
        w = w_ref[...]
        acc = b_ref[...] + w[SSD_CONV - 1:SSD_CONV] * x
        for kk in range(SSD_CONV - 1):
            off = 8 - (SSD_CONV - 1) + kk
            acc = acc + w[kk:kk + 1] * full[off:off + L]
        return acc * _sigmoid(acc)

    xs = conv_silu(xs_ref, xsh_ref, cwx_ref, cbx_ref)
    bc = conv_silu(bc_ref, bch_ref, cwbc_ref, cbbc_ref)
    bm = bc[:, :SSD_G * SSD_N]
    cm = bc[:, SSD_G * SSD_N:]

    pre = dt_ref[...] + dtb_ref[...]
    dt = jnp.maximum(pre, 0.0) + jnp.log(1.0 + jnp.exp(-jnp.abs(pre)))
    a = dt * (-jnp.exp(alog_ref[...]))
    cs = _cumsum_rows(a, L)
    cs_t = cs.T

    expand = expand_ref[...]

    def spread(x, parts):
        y = jnp.zeros((x.shape[0], inner), F32)
        for piece in _split_bf16(x, parts):
            y = y + jnp.dot(piece, expand, preferred_element_type=F32)
        return y

    dt_e = spread(dt, 3)
    cs_e = spread(cs, 3)
    cs_last_e = cs_e[L - 1:L, :]
    xdt = xs * dt_e
    xdt_b = xdt.astype(BF16)
    w_state = (xdt * jnp.exp(cs_last_e - cs_e)).astype(BF16)
    grow = jnp.exp(cs_e)
    dec_e = jnp.exp(cs_last_e)

    tril = (lax.broadcasted_iota(jnp.int32, (L, L), 0) >= lax.broadcasted_iota(jnp.int32, (L, L), 1))
    lane = lax.broadcasted_iota(jnp.int32, (1, 2 * SSD_P), 1)
    rpg = heads // SSD_G
    gw = rpg * SSD_P
    y_parts = []
    for g in range(SSD_G):
        cg = cm[:, g * SSD_N:(g + 1) * SSD_N].astype(BF16)
        bg = bm[:, g * SSD_N:(g + 1) * SSD_N].astype(BF16)
        cb = lax.dot_general(cg, bg, (((1,), (1,)), ((), ())), preferred_element_type=F32)
        st = state_ref[:, g * gw:(g + 1) * gw]
        y_off = jnp.dot(cg, st.astype(BF16), preferred_element_type=F32)
        new = jnp.dot(bg.T, w_state[:, g * gw:(g + 1) * gw], preferred_element_type=F32)
        state_ref[:, g * gw:(g + 1) * gw] = st * dec_e[:, g * gw:(g + 1) * gw] + new
        for pr in range(rpg // 2):
            c0 = g * gw + pr * 2 * SSD_P
            x_pair = xdt_b[:, c0:c0 + 2 * SSD_P]
            y_pair = jnp.zeros((L, 2 * SSD_P), F32)
            for half in range(2):
                h = g * rpg + pr * 2 + half
                diff = cs[:, h:h + 1] - cs_t[h:h + 1, :]
                decay = jnp.where(tril, jnp.exp(jnp.where(tril, diff, 0.0)), 0.0)
                m = (cb * decay).astype(BF16)
                keep = (lane >= SSD_P) if half else (lane < SSD_P)
                x_half = jnp.where(keep, x_pair, jnp.zeros_like(x_pair))
                y_pair = y_pair + jnp.dot(m, x_half, preferred_element_type=F32)
            y_parts.append(y_pair + y_off[:, pr * 2 * SSD_P:(pr + 1) * 2 * SSD_P]
                           * grow[:, c0:c0 + 2 * SSD_P])
    y = jnp.concatenate(y_parts, axis=-1)
    y = y + xs * dskip_ref[...]
    z = z_ref[...]
    y = y * (z * _sigmoid(z))
    o_ref[...] = (y * _rms_scale(y) * gain_ref[...]).astype(o_ref.dtype)


def _ssd(proj, dt_raw, conv_w, conv_b, dt_bias, a_log, d_skip, gain, z_blk, xs_blk):
    s = proj.shape[0]
    inner = gain.shape[0]
    heads = inner // SSD_P
    L = SSD_CHUNK
    assert 2 * SSD_G * SSD_N == inner and heads <= LANES

    def pad_heads(x):
        return jnp.pad(x.astype(F32), (0, LANES - heads)).reshape(1, LANES)

    expand = (jnp.arange(LANES)[:, None] == (jnp.arange(inner)[None, :] // SSD_P)).astype(BF16)
    halo = lambda blk: pl.BlockSpec((8, inner), lambda t, blk=blk: (jnp.maximum(t * (L // 8) - 1, 0), blk))
    main = lambda blk: pl.BlockSpec((L, inner), lambda t, blk=blk: (t, blk))
    row = lambda width: pl.BlockSpec((1, width), lambda t: (0, 0))
    return pl.pallas_call(
        _ssd_kernel,
        grid=(s // L,),
        in_specs=[
            main(z_blk), main(xs_blk), main(xs_blk + 1), halo(xs_blk), halo(xs_blk + 1),
            pl.BlockSpec((L, LANES), lambda t: (t, 0)),
            pl.BlockSpec((SSD_CONV, inner), lambda t: (0, 0)),
            pl.BlockSpec((SSD_CONV, inner), lambda t: (0, 1)),
            pl.BlockSpec((1, inner), lambda t: (0, 0)),
            pl.BlockSpec((1, inner), lambda t: (0, 1)),
            row(LANES), row(LANES), row(inner), row(inner),
            pl.BlockSpec((LANES, inner), lambda t: (0, 0)),
        ],
        out_specs=pl.BlockSpec((L, inner), lambda t: (t, 0)),
        out_shape=jax.ShapeDtypeStruct((s, inner), BF16),
        scratch_shapes=[pltpu.VMEM((SSD_N, inner), F32)],
        compiler_params=_params("arbitrary"),
        name="ssd",
    )(proj, proj, proj, proj, proj, dt_raw, conv_w, conv_w, conv_b.reshape(1, -1), conv_b.reshape(1, -1),
      pad_heads(dt_bias), pad_heads(a_log), jnp.repeat(d_skip.astype(F32), SSD_P).reshape(1, inner),
      gain.reshape(1, inner), expand)


def _logf_cumsum_kernel(f_ref, b_ref, o_ref, carry_ref, *, heads):
    @pl.when(pl.program_id(0) == 0)
    def _():
        carry_ref[...] = jnp.zeros_like(carry_ref)

    x = f_ref[...] + b_ref[...]
    ls = jnp.minimum(x, 0.0) - jnp.log(1.0 + jnp.exp(-jnp.abs(x)))
    cum = _cumsum_rows(ls, x.shape[0]) + carry_ref[...]
    carry_ref[...] = cum[x.shape[0] - 1:, :]
    lane = lax.broadcasted_iota(jnp.int32, cum.shape, 1)
    out = jnp.zeros_like(cum)
    for p, piece in enumerate(_split_bf16(cum, CUM_PIECES)):
        piece = jnp.where(lane < heads, piece.astype(F32), 0.0)
        out = out + (pltpu.roll(piece, p * heads, 1) if p else piece)
    o_ref[...] = out.astype(BF16)


def _logf_cumsum(f_logit, b_f, heads, *, tb=512):
    s = f_logit.shape[0]
    assert CUM_PIECES * heads <= LANES
    return pl.pallas_call(
        functools.partial(_logf_cumsum_kernel, heads=heads),
        grid=(s // tb,),
        in_specs=[pl.BlockSpec((tb, LANES), lambda t: (t, 0)), pl.BlockSpec((1, LANES), lambda t: (0, 0))],
        out_specs=pl.BlockSpec((tb, LANES), lambda t: (t, 0)),
        out_shape=jax.ShapeDtypeStruct((s, LANES), BF16),
        scratch_shapes=[pltpu.VMEM((1, LANES), F32)],
        compiler_params=_params("arbitrary"),
        name="logf_cumsum",
    )(f_logit, b_f)


def _attn_kernel(q_ref, k_ref, v_ref, c_ref, selq_ref, selk_ref, o_ref,
                 kaug_ref, vaug_ref, m_ref, acc_ref, *, tq, tk):
    i = pl.program_id(1)
    lane = lax.broadcasted_iota(jnp.int32, (1, LANES), 1)
    ones_k = ((lane >= FOX_HD) & (lane < FOX_HD + CUM_PIECES)).astype(F32)
    ones_q = ((lane >= FOX_HD + CUM_PIECES) & (lane < FOX_HD + 2 * CUM_PIECES)).astype(F32)
    ones_v = (lane == FOX_HD).astype(F32)

    c = c_ref[...]
    own = pl.ds(pl.multiple_of(i * tq, tq), tq)
    k_bias = jnp.dot(c, selk_ref[0], preferred_element_type=F32)
    kaug_ref[own, :] = (k_ref[...].astype(F32) + k_bias + ones_k).astype(BF16)
    vaug_ref[own, :] = (v_ref[...].astype(F32) + ones_v).astype(BF16)
    q_bias = jnp.dot(c, selq_ref[0], preferred_element_type=F32)
    q_aug = (q_ref[...].astype(F32) + q_bias + ones_q).astype(BF16)
    m_ref[...] = jnp.full_like(m_ref, NEG_INF)
    acc_ref[...] = jnp.zeros_like(acc_ref)

    def scores(j, row0):
        start = pl.multiple_of(j * tk, tk)
        return lax.dot_general(q_aug[row0:, :], kaug_ref[pl.ds(start, tk), :],
                               (((1,), (1,)), ((), ())), preferred_element_type=F32)

    def absorb(s, j, row0, masked):
        n = tq - row0
        if masked:
            qpos = lax.broadcasted_iota(jnp.int32, (n, tk), 0)
            kpos = lax.broadcasted_iota(jnp.int32, (n, tk), 1)
            s = jnp.where(kpos <= qpos, s, NEG_INF)
        rows = pl.ds(row0, n)
        v = vaug_ref[pl.ds(pl.multiple_of(j * tk, tk), tk), :]
        m_old = m_ref[rows, :]
        m_new = jnp.maximum(m_old, jnp.max(s, axis=-1, keepdims=True))
        p = jnp.exp(s - jnp.concatenate([m_new] * (tk // LANES), axis=1)).astype(BF16)
        acc_ref[rows, :] = (jnp.exp(m_old - m_new) * acc_ref[rows, :]
                            + jnp.dot(p, v, preferred_element_type=F32))
        m_ref[rows, :] = m_new

    past = i * (tq // tk)

    def body(j, carry):
        absorb(scores(j, 0), j, 0, False)
        return carry

    lax.fori_loop(0, past, body, 0)
    for rel in range(tq // tk):
        absorb(scores(past + rel, rel * tk), past + rel, rel * tk, True)
    acc = acc_ref[...]
    o_ref[...] = (acc / acc[:, FOX_HD:FOX_HD + 1]).astype(o_ref.dtype)


def _attention(qkv, cum, heads, *, tq=4096, tk=512):
    s = qkv.shape[0]
    assert s % tq == 0 and tq % tk == 0
    eye = jnp.eye(LANES, dtype=F32)
    src = jnp.arange(CUM_PIECES)[None, :] * heads + jnp.arange(heads)[:, None]
    pick = eye[src]

    def selector(dst0, sign):
        dst = eye[dst0 + jnp.arange(CUM_PIECES)]
        return (sign * jnp.einsum("hpa,pb->hab", pick, dst)).astype(BF16)

    selq = selector(FOX_HD, 1.0)
    selk = selector(FOX_HD + CUM_PIECES, -1.0)
    return pl.pallas_call(
        functools.partial(_attn_kernel, tq=tq, tk=tk),
        grid=(heads, s // tq),
        in_specs=[
            pl.BlockSpec((tq, LANES), lambda h, i: (i, h)),
            pl.BlockSpec((tq, LANES), lambda h, i: (i, heads + h)),
            pl.BlockSpec((tq, LANES), lambda h, i: (i, 2 * heads + h)),
            pl.BlockSpec((tq, LANES), lambda h, i: (i, 0)),
            pl.BlockSpec((1, LANES, LANES), lambda h, i: (h, 0, 0)),
            pl.BlockSpec((1, LANES, LANES), lambda h, i: (h, 0, 0)),
        ],
        out_specs=pl.BlockSpec((tq, LANES), lambda h, i: (i, h)),
        out_shape=jax.ShapeDtypeStruct((s, heads * LANES), BF16),
        scratch_shapes=[pltpu.VMEM((s, LANES), BF16), pltpu.VMEM((s, LANES), BF16),
                        pltpu.VMEM((tq, LANES), F32), pltpu.VMEM((tq, LANES), F32)],
        compiler_params=_params("arbitrary", "arbitrary"),
        name="fox_attention",
    )(qkv, qkv, qkv, cum, selq, selk)


def _pad_heads_cols(w, heads, scale=1.0):
    d = w.shape[0]
    w = (w * scale).reshape(d, heads, FOX_HD)
    return jnp.pad(w, ((0, 0), (0, 0), (0, LANES - FOX_HD))).reshape(d, heads * LANES)


def _pad_cols(w, width):
    return jnp.pad(w, ((0, 0), (0, width - w.shape[1])))


def kernel(x, ffn1_norm, ffn1_w_gate, ffn1_w_up, ffn1_w_down, mix_norm, hyb_w_in, hgrn_lb_logits,
           hgrn_out_gain, ssd_conv_w, ssd_conv_b, ssd_dt_bias, ssd_A_log, ssd_D, ssd_out_gain, hyb_w_out,
           fox_w_in, fox_b_f, fox_w_out, ffn2_norm, ffn2_w_gate, ffn2_w_up, ffn2_w_down, final_norm):
    bsz, s, d = x.shape
    depth = ffn1_norm.shape[0]
    hg_width = hgrn_out_gain.shape[1]
    inner = ssd_out_gain.shape[1]
    ssd_heads = ssd_dt_bias.shape[1]
    fox_heads = fox_b_f.shape[1]
    fox_width = fox_heads * FOX_HD
    hyb_main = 4 * hg_width + inner + inner + 2 * SSD_G * SSD_N
    assert hg_width == inner == d and hyb_w_in.shape[2] == hyb_main + ssd_heads

    bf = lambda w: w.astype(BF16)
    ffn1 = (bf(ffn1_w_gate), bf(ffn1_w_up), bf(ffn1_w_down))
    ffn2 = (bf(ffn2_w_gate), bf(ffn2_w_up), bf(ffn2_w_down))
    hyb_in, hyb_out = bf(hyb_w_in), bf(hyb_w_out)
    fox_in, fox_small, fox_out = [], [], []
    for j in range(fox_w_in.shape[0]):
        wq, wk, wv, wf = jnp.split(fox_w_in[j], (fox_width, 2 * fox_width, 3 * fox_width), axis=1)
        fox_in.append(bf(jnp.concatenate([_pad_heads_cols(wq, fox_heads, FOX_HD ** -0.5),
                                          _pad_heads_cols(wk, fox_heads),
                                          _pad_heads_cols(wv, fox_heads)], axis=1))[None])
        fox_small.append(_pad_cols(bf(wf), LANES))
        fox_out.append(bf(_pad_heads_cols(fox_w_out[j].T, fox_heads).T)[None])

    outs = []
    for b in range(bsz):
        h = x[b]
        for layer in range(depth):
            j = layer // 2
            h = _ffn(h, ffn1_norm[layer], *ffn1, layer)
            if layer % 2 == 0:
                proj, dt_raw = _norm_proj(h, mix_norm[layer], hyb_in, j, hyb_main,
                                          _pad_cols(hyb_in[j, :, hyb_main:], LANES), F32)
                o_a = _hgrn(proj, hgrn_lb_logits, hgrn_out_gain[j], j)
                o_b = _ssd(proj, dt_raw, ssd_conv_w[j], ssd_conv_b[j], ssd_dt_bias[j], ssd_A_log[j],
                           ssd_D[j], ssd_out_gain[j], z_blk=4 * hg_width // inner,
                           xs_blk=4 * hg_width // inner + 1)
                h = _out_proj(h, [o_a, o_b], hyb_out, j)
            else:
                qkv, f_logit = _norm_proj(h, mix_norm[layer], fox_in[j], 0, 3 * fox_heads * LANES,
                                          fox_small[j], BF16)
                cum = _logf_cumsum(f_logit, jnp.pad(fox_b_f[j], (0, LANES - fox_heads)).reshape(1, LANES),
                                   fox_heads)
                o = _attention(qkv, cum, fox_heads)
                h = _out_proj(h, [o], fox_out[j], 0)
            last = layer == depth - 1
            h = _ffn(h, ffn2_norm[layer], *ffn2, layer, final_gain=final_norm if last else None)
        outs.append(h)
    return jnp.stack(outs, axis=0)
```

```python
import functools

import jax
import jax.numpy as jnp
from jax import lax
from jax.experimental import pallas as pl
from jax.experimental.pallas import tpu as pltpu

F32 = jnp.float32
BF16 = jnp.bfloat16

EPS = 1e-6
TINY = 1e-20
NEG_INF = -1e30

LANES = 128
VMEM_LIMIT = 56 * 1024 * 1024

HG_DK = 128
HG_SUB = 16
SSD_P = 64
SSD_N = 128
SSD_G = 4
SSD_CHUNK = 128
SSD_CONV = 4
FOX_HD = 64
CUM_PIECES = 3


def _params(*sem):
    return pltpu.CompilerParams(dimension_semantics=sem, vmem_limit_bytes=VMEM_LIMIT)


def _rms_scale(x):
    return lax.rsqrt(jnp.mean(x * x, axis=-1, keepdims=True) + EPS)


def _sigmoid(x):
    return 1.0 / (1.0 + jnp.exp(-x))


def _cumsum_rows(x, period):
    row = lax.broadcasted_iota(jnp.int32, x.shape, 0) % period
    shift = 1
    while shift < period:
        x = x + jnp.where(row >= shift, pltpu.roll(x, shift, 0), 0.0)
        shift *= 2
    return x


def _ffn_kernel(h_ref, g_ref, wg_ref, wu_ref, wd_ref, fg_ref, o_ref, u_ref, acc_ref, *, tf, final_norm):
    x = h_ref[...]
    u_ref[...] = (x * _rms_scale(x) * g_ref[...]).astype(BF16)
    ff = wg_ref.shape[1]
    for c in range(ff // tf):
        u = u_ref[...]
        cols = slice(c * tf, (c + 1) * tf)
        gate = jnp.dot(u, wg_ref[:, cols], preferred_element_type=F32)
        up = jnp.dot(u, wu_ref[:, cols], preferred_element_type=F32)
        act = (gate * _sigmoid(gate) * up).astype(BF16)
        part = jnp.dot(act, wd_ref[cols, :], preferred_element_type=F32)
        if c == 0:
            acc_ref[...] = part
        else:
            acc_ref[...] += part
    y = h_ref[...] + 0.5 * acc_ref[...]
    if final_norm:
        y = y * _rms_scale(y) * fg_ref[...]
    o_ref[...] = y


def _ffn(h, gain, wg, wu, wd, layer, final_gain=None, *, tm=1024, tf=256):
    s, d = h.shape
    ff = wg.shape[2]
    assert s % tm == 0 and ff % tf == 0
    final_norm = final_gain is not None
    fg = final_gain if final_norm else gain
    resident = lambda shape: pl.BlockSpec((None,) + shape, lambda i: (layer, 0, 0),
                                          pipeline_mode=pl.Buffered(1))
    return pl.pallas_call(
        functools.partial(_ffn_kernel, tf=tf, final_norm=final_norm),
        grid=(s // tm,),
        in_specs=[
            pl.BlockSpec((tm, d), lambda i: (i, 0)),
            pl.BlockSpec((1, d), lambda i: (0, 0)),
            resident((d, ff)), resident((d, ff)), resident((ff, d)),
            pl.BlockSpec((1, d), lambda i: (0, 0)),
        ],
        out_specs=pl.BlockSpec((tm, d), lambda i: (i, 0)),
        out_shape=jax.ShapeDtypeStruct((s, d), F32),
        scratch_shapes=[pltpu.VMEM((tm, d), BF16), pltpu.VMEM((tm, d), F32)],
        compiler_params=_params("parallel"),
        name="ffn",
    )(h, gain.reshape(1, d), wg, wu, wd, fg.reshape(1, d))


def _proj_kernel(h_ref, g_ref, w_ref, ws_ref, o_ref, os_ref, u_ref):
    j = pl.program_id(1)

    @pl.when(j == 0)
    def _():
        x = h_ref[...]
        u = (x * _rms_scale(x) * g_ref[...]).astype(BF16)
        u_ref[...] = u
        os_ref[...] = jnp.dot(u, ws_ref[...], preferred_element_type=F32)

    o_ref[...] = jnp.dot(u_ref[...], w_ref[...], preferred_element_type=F32).astype(o_ref.dtype)


def _norm_proj(h, gain, w_main, layer, n, w_small, out_dtype, *, tm=1024, tn=1024):
    s, d = h.shape
    assert s % tm == 0 and n % tn == 0 and n <= w_main.shape[2]
    return pl.pallas_call(
        _proj_kernel,
        grid=(s // tm, n // tn),
        in_specs=[
            pl.BlockSpec((tm, d), lambda i, j: (i, 0)),
            pl.BlockSpec((1, d), lambda i, j: (0, 0)),
            pl.BlockSpec((None, d, tn), lambda i, j: (layer, 0, j)),
            pl.BlockSpec((d, LANES), lambda i, j: (0, 0)),
        ],
        out_specs=[
            pl.BlockSpec((tm, tn), lambda i, j: (i, j)),
            pl.BlockSpec((tm, LANES), lambda i, j: (i, 0)),
        ],
        out_shape=[jax.ShapeDtypeStruct((s, n), out_dtype), jax.ShapeDtypeStruct((s, LANES), F32)],
        scratch_shapes=[pltpu.VMEM((tm, d), BF16)],
        compiler_params=_params("parallel", "arbitrary"),
        name="norm_proj",
    )(h, gain.reshape(1, d), w_main, w_small)


def _out_proj_kernel(*refs, n_in):
    h_ref = refs[0]
    x_refs = refs[1:1 + n_in]
    w_refs = refs[1 + n_in:1 + 2 * n_in]
    o_ref = refs[1 + 2 * n_in]
    y = h_ref[...]
    for x_ref, w_ref in zip(x_refs, w_refs):
        y = y + jnp.dot(x_ref[...], w_ref[...], preferred_element_type=F32)
    o_ref[...] = y


def _out_proj(h, xs, w, layer, *, tm=1024):
    s, d = h.shape
    n_in = len(xs)
    in_specs = [pl.BlockSpec((tm, d), lambda i: (i, 0))]
    in_specs += [pl.BlockSpec((tm, x.shape[1]), lambda i: (i, 0)) for x in xs]
    row = 0
    for x in xs:
        width = x.shape[1]
        assert row % width == 0
        in_specs.append(pl.BlockSpec((None, width, d), lambda i, blk=row // width: (layer, blk, 0)))
        row += width
    assert row == w.shape[1]
    return pl.pallas_call(
        functools.partial(_out_proj_kernel, n_in=n_in),
        grid=(s // tm,),
        in_specs=in_specs,
        out_specs=pl.BlockSpec((tm, d), lambda i: (i, 0)),
        out_shape=jax.ShapeDtypeStruct((s, d), F32),
        compiler_params=_params("parallel"),
        name="out_proj",
    )(h, *xs, *([w] * n_in))


def _hgrn_kernel(q_ref, f_ref, v_ref, g_ref, lbl_ref, gain_ref, o_ref,
                 state_ref, kv_ref, sprev_ref, *, layer_j, tb):
    c = HG_SUB
    nb = tb // c

    @pl.when(pl.program_id(1) == 0)
    def _():
        state_ref[...] = jnp.zeros_like(state_ref)

    logits = lbl_ref[...]
    e = jnp.exp(logits - jnp.max(logits, axis=0, keepdims=True))
    p = e / jnp.sum(e, axis=0, keepdims=True)
    lb = jnp.zeros((1, HG_DK), F32)
    for i in range(1, layer_j + 1):
        lb = lb + p[i:i + 1]

    fl = f_ref[...]
    f = lb + (1.0 - lb) * _sigmoid(fl)
    logf = jnp.log(jnp.maximum(f, TINY))
    k = (1.0 - lb) * _sigmoid(-fl)
    q = q_ref[...]
    v = v_ref[...]

    b = _cumsum_rows(logf, c)
    b3 = b.reshape(nb, c, HG_DK)
    q3 = q.reshape(nb, c, HG_DK)
    k3 = k.reshape(nb, c, HG_DK)
    v3 = v.reshape(nb, c, HG_DK)
    b_last = b3[:, c - 1:c, :]

    kd = (k3 * jnp.exp(b_last - b3)).astype(BF16)
    kv_ref[...] = lax.dot_general(v3.astype(BF16), kd, (((1,), (1,)), ((0,), (0,))),
                                  preferred_element_type=F32)
    dec = jnp.exp(b_last)

    s_cur = state_ref[...]
    for n in range(nb):
        sprev_ref[n] = s_cur.astype(BF16)
        s_cur = s_cur * dec[n] + kv_ref[n]
    state_ref[...] = s_cur

    qd = (q3 * jnp.exp(b3)).astype(BF16)
    o = lax.dot_general(qd, sprev_ref[...], (((2,), (2,)), ((0,), (0,))),
                        preferred_element_type=F32)

    row = lax.broadcasted_iota(jnp.int32, (nb, c, 1), 1)
    for s in range(c):
        live = row >= s
        diff = b3 - b3[:, s:s + 1, :]
        w = jnp.exp(jnp.where(live, diff, 0.0))
        col = jnp.sum(q3 * w * k3[:, s:s + 1, :], axis=-1, keepdims=True)
        o = o + jnp.where(live, col, 0.0) * v3[:, s:s + 1, :]

    o = o.reshape(tb, HG_DK)
    g = g_ref[...]
    o_ref[...] = (o * _rms_scale(o) * gain_ref[...] * (g * _sigmoid(g))).astype(o_ref.dtype)


def _hgrn(proj, lb_logits, gain, layer_j, *, tb=1024):
    s = proj.shape[0]
    heads = gain.shape[0] // HG_DK
    n_even = lb_logits.shape[0]
    nb = tb // HG_SUB
    blk = lambda off: pl.BlockSpec((tb, HG_DK), lambda h, t, off=off: (t, off + h))
    return pl.pallas_call(
        functools.partial(_hgrn_kernel, layer_j=layer_j, tb=tb),
        grid=(heads, s // tb),
        in_specs=[
            blk(0), blk(heads), blk(2 * heads), blk(3 * heads),
            pl.BlockSpec((n_even, HG_DK), lambda h, t: (0, h)),
            pl.BlockSpec((1, HG_DK), lambda h, t: (0, h)),
        ],
        out_specs=pl.BlockSpec((tb, HG_DK), lambda h, t: (t, h)),
        out_shape=jax.ShapeDtypeStruct((s, heads * HG_DK), BF16),
        scratch_shapes=[
            pltpu.VMEM((HG_DK, HG_DK), F32),
            pltpu.VMEM((nb, HG_DK, HG_DK), F32),
            pltpu.VMEM((nb, HG_DK, HG_DK), BF16),
        ],
        compiler_params=_params("parallel", "arbitrary"),
        name="hgrn2",
    )(proj, proj, proj, proj, lb_logits, gain.reshape(1, -1))


def _split_bf16(x, parts):
    out = []
    for _ in range(parts):
        hi = x.astype(BF16)
        out.append(hi)
        x = x - hi.astype(F32)
    return out


def _ssd_kernel(z_ref, xs_ref, bc_ref, xsh_ref, bch_ref, dt_ref, cwx_ref, cwbc_ref, cbx_ref, cbbc_ref,
                dtb_ref, alog_ref, dskip_ref, gain_ref, expand_ref, o_ref, state_ref):
    L = SSD_CHUNK
    inner = z_ref.shape[1]
    heads = inner // SSD_P
    t = pl.program_id(0)

    @pl.when(t == 0)
    def _():
        state_ref[...] = jnp.zeros_like(state_ref)

    halo_on = (t > 0).astype(F32)

    def conv_silu(x_ref, halo_ref, w_ref, b_ref):
        x = x_ref[...]
        full = jnp.concatenate([halo_ref[...] * halo_on, x], axis=0)
        w = w_ref[...]
        acc = b_ref[...] + w[SSD_CONV - 1:SSD_CONV] * x
        for kk in range(SSD_CONV - 1):
            off = 8 - (SSD_CONV - 1) + kk
            acc = acc + w[kk:kk + 1] * full[off:off + L]
        return acc * _sigmoid(acc)

    xs = conv_silu(xs_ref, xsh_ref, cwx_ref, cbx_ref)
    bc = conv_silu(bc_ref, bch_ref, cwbc_ref, cbbc_ref)
    bm = bc[:, :SSD_G * SSD_N]
    cm = bc[:, SSD_G * SSD_N:]

    pre = dt_ref[...] + dtb_ref[...]
    dt = jnp.maximum(pre, 0.0) + jnp.log(1.0 + jnp.exp(-jnp.abs(pre)))
    a = dt * (-jnp.exp(alog_ref[...]))
    cs = _cumsum_rows(a, L)
    cs_t = cs.T

    expand = expand_ref[...]

    def spread(x, parts):
        y = jnp.zeros((x.shape[0], inner), F32)
        for piece in _split_bf16(x, parts):
            y = y + jnp.dot(piece, expand, preferred_element_type=F32)
        return y

    dt_e = spread(dt, 3)
    cs_e = spread(cs, 3)
    cs_last_e = cs_e[L - 1:L, :]
    xdt = xs * dt_e
    xdt_b = xdt.astype(BF16)
    w_state = (xdt * jnp.exp(cs_last_e - cs_e)).astype(BF16)
    grow = jnp.exp(cs_e)
    dec_e = jnp.exp(cs_last_e)

    tril = (lax.broadcasted_iota(jnp.int32, (L, L), 0) >= lax.broadcasted_iota(jnp.int32, (L, L), 1))
    lane = lax.broadcasted_iota(jnp.int32, (1, 2 * SSD_P), 1)
    rpg = heads // SSD_G
    gw = rpg * SSD_P
    y_parts = []
    for g in range(SSD_G):
        cg = cm[:, g * SSD_N:(g + 1) * SSD_N].astype(BF16)
        bg = bm[:, g * SSD_N:(g + 1) * SSD_N].astype(BF16)
        cb = lax.dot_general(cg, bg, (((1,), (1,)), ((), ())), preferred_element_type=F32)
        st = state_ref[:, g * gw:(g + 1) * gw]
        y_off = jnp.dot(cg, st.astype(BF16), preferred_element_type=F32)
        new = jnp.dot(bg.T, w_state[:, g * gw:(g + 1) * gw], preferred_element_type=F32)
        state_ref[:, g * gw:(g + 1) * gw] = st * dec_e[:, g * gw:(g + 1) * gw] + new
        for pr in range(rpg // 2):
            c0 = g * gw + pr * 2 * SSD_P
            x_pair = xdt_b[:, c0:c0 + 2 * SSD_P]
            y_pair = jnp.zeros((L, 2 * SSD_P), F32)
            for half in range(2):
                h = g * rpg + pr * 2 + half
                diff = cs[:, h:h + 1] - cs_t[h:h + 1, :]
                decay = jnp.where(tril, jnp.exp(jnp.where(tril, diff, 0.0)), 0.0)
                m = (cb * decay).astype(BF16)
                keep = (lane >= SSD_P) if half else (lane < SSD_P)
                x_half = jnp.where(keep, x_pair, jnp.zeros_like(x_pair))
                y_pair = y_pair + jnp.dot(m, x_half, preferred_element_type=F32)
            y_parts.append(y_pair + y_off[:, pr * 2 * SSD_P:(pr + 1) * 2 * SSD_P]
                           * grow[:, c0:c0 + 2 * SSD_P])
    y = jnp.concatenate(y_parts, axis=-1)
    y = y + xs * dskip_ref[...]
    z = z_ref[...]
    y = y * (z * _sigmoid(z))
    o_ref[...] = (y * _rms_scale(y) * gain_ref[...]).astype(o_ref.dtype)


def _ssd(proj, dt_raw, conv_w, conv_b, dt_bias, a_log, d_skip, gain, z_blk, xs_blk):
    s = proj.shape[0]
    inner = gain.shape[0]
    heads = inner // SSD_P
    L = SSD_CHUNK
    assert 2 * SSD_G * SSD_N == inner and heads <= LANES

    def pad_heads(x):
        return jnp.pad(x.astype(F32), (0, LANES - heads)).reshape(1, LANES)

    expand = (jnp.arange(LANES)[:, None] == (jnp.arange(inner)[None, :] // SSD_P)).astype(BF16)
    halo = lambda blk: pl.BlockSpec((8, inner), lambda t, blk=blk: (jnp.maximum(t * (L // 8) - 1, 0), blk))
    main = lambda blk: pl.BlockSpec((L, inner), lambda t, blk=blk: (t, blk))
    row = lambda width: pl.BlockSpec((1, width), lambda t: (0, 0))
    return pl.pallas_call(
        _ssd_kernel,
        grid=(s // L,),
        in_specs=[
            main(z_blk), main(xs_blk), main(xs_blk + 1), halo(xs_blk), halo(xs_blk + 1),
            pl.BlockSpec((L, LANES), lambda t: (t, 0)),
            pl.BlockSpec((SSD_CONV, inner), lambda t: (0, 0)),
            pl.BlockSpec((SSD_CONV, inner), lambda t: (0, 1)),
            pl.BlockSpec((1, inner), lambda t: (0, 0)),
            pl.BlockSpec((1, inner), lambda t: (0, 1)),
            row(LANES), row(LANES), row(inner), row(inner),
            pl.BlockSpec((LANES, inner), lambda t: (0, 0)),
        ],
        out_specs=pl.BlockSpec((L, inner), lambda t: (t, 0)),
        out_shape=jax.ShapeDtypeStruct((s, inner), BF16),
        scratch_shapes=[pltpu.VMEM((SSD_N, inner), F32)],
        compiler_params=_params("arbitrary"),
        name="ssd",
    )(proj, proj, proj, proj, proj, dt_raw, conv_w, conv_w, conv_b.reshape(1, -1), conv_b.reshape(1, -1),
      pad_heads(dt_bias), pad_heads(a_log), jnp.repeat(d_skip.astype(F32), SSD_P).reshape(1, inner),
      gain.reshape(1, inner), expand)


def _logf_cumsum_kernel(f_ref, b_ref, o_ref, carry_ref, *, heads):
    @pl.when(pl.program_id(0) == 0)
    def _():
        carry_ref[...] = jnp.zeros_like(carry_ref)

    x = f_ref[...] + b_ref[...]
    ls = jnp.minimum(x, 0.0) - jnp.log(1.0 + jnp.exp(-jnp.abs(x)))
    cum = _cumsum_rows(ls, x.shape[0]) + carry_ref[...]
    carry_ref[...] = cum[x.shape[0] - 1:, :]
    lane = lax.broadcasted_iota(jnp.int32, cum.shape, 1)
    out = jnp.zeros_like(cum)
    for p, piece in enumerate(_split_bf16(cum, CUM_PIECES)):
        piece = jnp.where(lane < heads, piece.astype(F32), 0.0)
        out = out + (pltpu.roll(piece, p * heads, 1) if p else piece)
    o_ref[...] = out.astype(BF16)


def _logf_cumsum(f_logit, b_f, heads, *, tb=512):
    s = f_logit.shape[0]
    assert CUM_PIECES * heads <= LANES
    return pl.pallas_call(
        functools.partial(_logf_cumsum_kernel, heads=heads),
        grid=(s // tb,),
        in_specs=[pl.BlockSpec((tb, LANES), lambda t: (t, 0)), pl.BlockSpec((1, LANES), lambda t: (0, 0))],
        out_specs=pl.BlockSpec((tb, LANES), lambda t: (t, 0)),
        out_shape=jax.ShapeDtypeStruct((s, LANES), BF16),
        scratch_shapes=[pltpu.VMEM((1, LANES), F32)],
        compiler_params=_params("arbitrary"),
        name="logf_cumsum",
    )(f_logit, b_f)


def _attn_kernel(q_ref, k_ref, v_ref, c_ref, selq_ref, selk_ref, o_ref,
                 kaug_ref, vaug_ref, m_ref, acc_ref, first_ref, *, tq, tk):
    i = pl.program_id(1)
    lane = lax.broadcasted_iota(jnp.int32, (1, LANES), 1)
    c = c_ref[...]
    own = pl.ds(pl.multiple_of(i * tq, tq), tq)
    past = i * (tq // tk)

    def one_head(side, carry):
        data = (lane >= side * FOX_HD) & (lane < (side + 1) * FOX_HD)
        spare = FOX_HD * (1 - side)
        in_span = lambda lo, n: ((lane >= lo) & (lane < lo + n)).astype(F32)
        ones_k = in_span(spare, CUM_PIECES)
        ones_q = in_span(spare + CUM_PIECES, CUM_PIECES)
        ones_v = in_span(spare, 1)

        k_bias = jnp.dot(c, selk_ref[0, side], preferred_element_type=F32)
        kaug_ref[side, own, :] = (jnp.where(data, k_ref[...].astype(F32), 0.0) + k_bias + ones_k).astype(BF16)
        vaug_ref[side, own, :] = (jnp.where(data, v_ref[...].astype(F32), 0.0) + ones_v).astype(BF16)
        q_bias = jnp.dot(c, selq_ref[0, side], preferred_element_type=F32)
        q_aug = (jnp.where(data, q_ref[...].astype(F32), 0.0) + q_bias + ones_q).astype(BF16)
        m_ref[...] = jnp.full_like(m_ref, NEG_INF)
        acc_ref[...] = jnp.zeros_like(acc_ref)

        def scores(j, row0):
            start = pl.multiple_of(j * tk, tk)
            return lax.dot_general(q_aug[row0:, :], kaug_ref[side, pl.ds(start, tk), :],
                                   (((1,), (1,)), ((), ())), preferred_element_type=F32)

        def absorb(s, j, row0, masked):
            n = tq - row0
            if masked:
                qpos = lax.broadcasted_iota(jnp.int32, (n, tk), 0)
                kpos = lax.broadcasted_iota(jnp.int32, (n, tk), 1)
                s = jnp.where(kpos <= qpos, s, NEG_INF)
            rows = pl.ds(row0, n)
            v = vaug_ref[side, pl.ds(pl.multiple_of(j * tk, tk), tk), :]
            m_old = m_ref[rows, :]
            m_new = jnp.maximum(m_old, jnp.max(s, axis=-1, keepdims=True))
            p = jnp.exp(s - jnp.concatenate([m_new] * (tk // LANES), axis=1)).astype(BF16)
            acc_ref[rows, :] = (jnp.exp(m_old - m_new) * acc_ref[rows, :]
                                + jnp.dot(p, v, preferred_element_type=F32))
            m_ref[rows, :] = m_new

        def body(j, inner):
            absorb(scores(j, 0), j, 0, False)
            return inner

        lax.fori_loop(0, past, body, 0)
        for rel in range(tq // tk):
            absorb(scores(past + rel, rel * tk), past + rel, rel * tk, True)
        acc = acc_ref[...]
        out = acc / jnp.sum(acc * ones_v, axis=-1, keepdims=True)

        @pl.when(side == 0)
        def _():
            first_ref[...] = out

        @pl.when(side == 1)
        def _():
            o_ref[...] = jnp.where(data, out, first_ref[...]).astype(o_ref.dtype)

        return carry

    lax.fori_loop(0, 2, one_head, 0)


def _attention(qkv, cum, heads, *, tq=4096, tk=512):
    s = qkv.shape[0]
    assert s % tq == 0 and tq % tk == 0 and heads % 2 == 0 and 2 * FOX_HD == LANES
    pairs = heads // 2
    eye = jnp.eye(LANES, dtype=F32)

    def selector(offset, sign):
        rows = []
        for h in range(heads):
            spare = FOX_HD * (1 - h % 2)
            src = eye[jnp.arange(CUM_PIECES) * heads + h]
            dst = eye[spare + offset + jnp.arange(CUM_PIECES)]
            rows.append(sign * jnp.einsum("pa,pb->ab", src, dst))
        return jnp.stack(rows).reshape(pairs, 2, LANES, LANES).astype(BF16)

    selq = selector(0, 1.0)
    selk = selector(CUM_PIECES, -1.0)
    blk = lambda off: pl.BlockSpec((tq, LANES), lambda p, i, off=off: (i, off + p))
    sel = pl.BlockSpec((1, 2, LANES, LANES), lambda p, i: (p, 0, 0, 0))
    return pl.pallas_call(
        functools.partial(_attn_kernel, tq=tq, tk=tk),
        grid=(pairs, s // tq),
        in_specs=[blk(0), blk(pairs), blk(2 * pairs),
                  pl.BlockSpec((tq, LANES), lambda p, i: (i, 0)), sel, sel],
        out_specs=pl.BlockSpec((tq, LANES), lambda p, i: (i, p)),
        out_shape=jax.ShapeDtypeStruct((s, heads * FOX_HD), BF16),
        scratch_shapes=[pltpu.VMEM((2, s, LANES), BF16), pltpu.VMEM((2, s, LANES), BF16),
                        pltpu.VMEM((tq, LANES), F32), pltpu.VMEM((tq, LANES), F32),
                        pltpu.VMEM((tq, LANES), F32)],
        compiler_params=_params("arbitrary", "arbitrary"),
        name="fox_attention",
    )(qkv, qkv, qkv, cum, selq, selk)


def _pad_cols(w, width):
    return jnp.pad(w, ((0, 0), (0, width - w.shape[1])))


def kernel(x, ffn1_norm, ffn1_w_gate, ffn1_w_up, ffn1_w_down, mix_norm, hyb_w_in, hgrn_lb_logits,
           hgrn_out_gain, ssd_conv_w, ssd_conv_b, ssd_dt_bias, ssd_A_log, ssd_D, ssd_out_gain, hyb_w_out,
           fox_w_in, fox_b_f, fox_w_out, ffn2_norm, ffn2_w_gate, ffn2_w_up, ffn2_w_down, final_norm):
    bsz, s, d = x.shape
    depth = ffn1_norm.shape[0]
    hg_width = hgrn_out_gain.shape[1]
    inner = ssd_out_gain.shape[1]
    ssd_heads = ssd_dt_bias.shape[1]
    fox_heads = fox_b_f.shape[1]
    fox_width = fox_heads * FOX_HD
    hyb_main = 4 * hg_width + inner + inner + 2 * SSD_G * SSD_N
    assert hg_width == inner == d and hyb_w_in.shape[2] == hyb_main + ssd_heads

    bf = lambda w: w.astype(BF16)
    ffn1 = (bf(ffn1_w_gate), bf(ffn1_w_up), bf(ffn1_w_down))
    ffn2 = (bf(ffn2_w_gate), bf(ffn2_w_up), bf(ffn2_w_down))
    hyb_in, hyb_out = bf(hyb_w_in), bf(hyb_w_out)
    col_scale = jnp.where(jnp.arange(fox_w_in.shape[2]) < fox_width, FOX_HD ** -0.5, 1.0)
    fox_in = bf(fox_w_in * col_scale)
    fox_out = bf(fox_w_out)

    outs = []
    for b in range(bsz):
        h = x[b]
        for layer in range(depth):
            j = layer // 2
            h = _ffn(h, ffn1_norm[layer], *ffn1, layer)
            if layer % 2 == 0:
                proj, dt_raw = _norm_proj(h, mix_norm[layer], hyb_in, j, hyb_main,
                                          _pad_cols(hyb_in[j, :, hyb_main:], LANES), F32)
                o_a = _hgrn(proj, hgrn_lb_logits, hgrn_out_gain[j], j)
                o_b = _ssd(proj, dt_raw, ssd_conv_w[j], ssd_conv_b[j], ssd_dt_bias[j], ssd_A_log[j],
                           ssd_D[j], ssd_out_gain[j], z_blk=4 * hg_width // inner,
                           xs_blk=4 * hg_width // inner + 1)
                h = _out_proj(h, [o_a, o_b], hyb_out, j)
            else:
                qkv, f_logit = _norm_proj(h, mix_norm[layer], fox_in, j, 3 * fox_width,
                                          _pad_cols(fox_in[j, :, 3 * fox_width:], LANES), BF16)
                cum = _logf_cumsum(f_logit, jnp.pad(fox_b_f[j], (0, LANES - fox_heads)).reshape(1, LANES),
                                   fox_heads)
                o = _attention(qkv, cum, fox_heads)
                h = _out_proj(h, [o], fox_out, j)
            last = layer == depth - 1
            h = _ffn(h, ffn2_norm[layer], *ffn2, layer, final_gain=final_norm if last else None)
        outs.append(h)
    return jnp.stack(outs, axis=0)
```

```python
import functools

import jax
import jax.numpy as jnp
from jax import lax
from jax.experimental import pallas as pl
from jax.experimental.pallas import tpu as pltpu

F32 = jnp.float32
BF16 = jnp.bfloat16

EPS = 1e-6
TINY = 1e-20
NEG_INF = -1e30

LANES = 128
VMEM_LIMIT = 56 * 1024 * 1024

HG_DK = 128
HG_SUB = 16
SSD_P = 64
SSD_N = 128
SSD_G = 4
SSD_CHUNK = 128
SSD_CONV = 4
FOX_HD = 64
CUM_PIECES = 3


def _params(*sem):
    return pltpu.CompilerParams(dimension_semantics=sem, vmem_limit_bytes=VMEM_LIMIT)


def _rms_scale(x):
    return lax.rsqrt(jnp.mean(x * x, axis=-1, keepdims=True) + EPS)


def _sigmoid(x):
    return 1.0 / (1.0 + jnp.exp(-x))


def _cumsum_rows(x, period):
    row = lax.broadcasted_iota(jnp.int32, x.shape, 0) % period
    shift = 1
    while shift < period:
        x = x + jnp.where(row >= shift, pltpu.roll(x, shift, 0), 0.0)
        shift *= 2
    return x


def _ffn_kernel(h_ref, g_ref, wg_ref, wu_ref, wd_ref, fg_ref, o_ref, u_ref, acc_ref, *, tf, final_norm):
    x = h_ref[...]
    u_ref[...] = (x * _rms_scale(x) * g_ref[...]).astype(BF16)
    ff = wg_ref.shape[1]
    for c in range(ff // tf):
        u = u_ref[...]
        cols = slice(c * tf, (c + 1) * tf)
        gate = jnp.dot(u, wg_ref[:, cols], preferred_element_type=F32)
        up = jnp.dot(u, wu_ref[:, cols], preferred_element_type=F32)
        act = (gate * _sigmoid(gate) * up).astype(BF16)
        part = jnp.dot(act, wd_ref[cols, :], preferred_element_type=F32)
        if c == 0:
            acc_ref[...] = part
        else:
            acc_ref[...] += part
    y = h_ref[...] + 0.5 * acc_ref[...]
    if final_norm:
        y = y * _rms_scale(y) * fg_ref[...]
    o_ref[...] = y


def _ffn(h, gain, wg, wu, wd, layer, final_gain=None, *, tm=1024, tf=256):
    s, d = h.shape
    ff = wg.shape[2]
    assert s % tm == 0 and ff % tf == 0
    final_norm = final_gain is not None
    fg = final_gain if final_norm else gain
    resident = lambda shape: pl.BlockSpec((None,) + shape, lambda i: (layer, 0, 0),
                                          pipeline_mode=pl.Buffered(1))
    return pl.pallas_call(
        functools.partial(_ffn_kernel, tf=tf, final_norm=final_norm),
        grid=(s // tm,),
        in_specs=[
            pl.BlockSpec((tm, d), lambda i: (i, 0)),
            pl.BlockSpec((1, d), lambda i: (0, 0)),
            resident((d, ff)), resident((d, ff)), resident((ff, d)),
            pl.BlockSpec((1, d), lambda i: (0, 0)),
        ],
        out_specs=pl.BlockSpec((tm, d), lambda i: (i, 0)),
        out_shape=jax.ShapeDtypeStruct((s, d), F32),
        scratch_shapes=[pltpu.VMEM((tm, d), BF16), pltpu.VMEM((tm, d), F32)],
        compiler_params=_params("parallel"),
        name="ffn",
    )(h, gain.reshape(1, d), wg, wu, wd, fg.reshape(1, d))


def _proj_kernel(h_ref, g_ref, w_ref, ws_ref, o_ref, os_ref, u_ref, *, tn):
    x = h_ref[...]
    u_ref[...] = (x * _rms_scale(x) * g_ref[...]).astype(BF16)
    os_ref[...] = jnp.dot(u_ref[...], ws_ref[...], preferred_element_type=F32)
    for c in range(o_ref.shape[1] // tn):
        cols = slice(c * tn, (c + 1) * tn)
        o_ref[:, cols] = jnp.dot(u_ref[...], w_ref[:, cols], preferred_element_type=F32).astype(o_ref.dtype)


def _norm_proj(h, gain, w_main, layer, n, w_small, out_dtype, *, tm=512, tn=1024):
    s, d = h.shape
    assert s % tm == 0 and n % tn == 0 and n <= w_main.shape[2]
    return pl.pallas_call(
        functools.partial(_proj_kernel, tn=tn),
        grid=(s // tm,),
        in_specs=[
            pl.BlockSpec((tm, d), lambda i: (i, 0)),
            pl.BlockSpec((1, d), lambda i: (0, 0)),
            pl.BlockSpec((None, d, n), lambda i: (layer, 0, 0), pipeline_mode=pl.Buffered(1)),
            pl.BlockSpec((d, LANES), lambda i: (0, 0)),
        ],
        out_specs=[
            pl.BlockSpec((tm, n), lambda i: (i, 0)),
            pl.BlockSpec((tm, LANES), lambda i: (i, 0)),
        ],
        out_shape=[jax.ShapeDtypeStruct((s, n), out_dtype), jax.ShapeDtypeStruct((s, LANES), F32)],
        scratch_shapes=[pltpu.VMEM((tm, d), BF16)],
        compiler_params=_params("parallel"),
        name="norm_proj",
    )(h, gain.reshape(1, d), w_main, w_small)


def _out_proj_kernel(*refs, n_in):
    h_ref = refs[0]
    x_refs = refs[1:1 + n_in]
    w_refs = refs[1 + n_in:1 + 2 * n_in]
    o_ref = refs[1 + 2 * n_in]
    y = h_ref[...]
    for x_ref, w_ref in zip(x_refs, w_refs):
        y = y + jnp.dot(x_ref[...], w_ref[...], preferred_element_type=F32)
    o_ref[...] = y


def _out_proj(h, xs, w, layer, *, tm=1024):
    s, d = h.shape
    n_in = len(xs)
    in_specs = [pl.BlockSpec((tm, d), lambda i: (i, 0))]
    in_specs += [pl.BlockSpec((tm, x.shape[1]), lambda i: (i, 0)) for x in xs]
    row = 0
    for x in xs:
        width = x.shape[1]
        assert row % width == 0
        in_specs.append(pl.BlockSpec((None, width, d), lambda i, blk=row // width: (layer, blk, 0)))
        row += width
    assert row == w.shape[1]
    return pl.pallas_call(
        functools.partial(_out_proj_kernel, n_in=n_in),
        grid=(s // tm,),
        in_specs=in_specs,
        out_specs=pl.BlockSpec((tm, d), lambda i: (i, 0)),
        out_shape=jax.ShapeDtypeStruct((s, d), F32),
        compiler_params=_params("parallel"),
        name="out_proj",
    )(h, *xs, *([w] * n_in))


def _hgrn_kernel(q_ref, f_ref, v_ref, g_ref, lbl_ref, gain_ref, o_ref,
                 state_ref, kv_ref, sprev_ref, *, layer_j, tb):
    c = HG_SUB
    nb = tb // c

    @pl.when(pl.program_id(1) == 0)
    def _():
        state_ref[...] = jnp.zeros_like(state_ref)

    logits = lbl_ref[...]
    e = jnp.exp(logits - jnp.max(logits, axis=0, keepdims=True))
    p = e / jnp.sum(e, axis=0, keepdims=True)
    lb = jnp.zeros((1, HG_DK), F32)
    for i in range(1, layer_j + 1):
        lb = lb + p[i:i + 1]

    fl = f_ref[...]
    f = lb + (1.0 - lb) * _sigmoid(fl)
    logf = jnp.log(jnp.maximum(f, TINY))
    k = (1.0 - lb) * _sigmoid(-fl)
    q = q_ref[...]
    v = v_ref[...]

    b = _cumsum_rows(logf, c)
    b3 = b.reshape(nb, c, HG_DK)
    q3 = q.reshape(nb, c, HG_DK)
    k3 = k.reshape(nb, c, HG_DK)
    v3 = v.reshape(nb, c, HG_DK)
    b_last = b3[:, c - 1:c, :]

    kd = (k3 * jnp.exp(b_last - b3)).astype(BF16)
    kv_ref[...] = lax.dot_general(v3.astype(BF16), kd, (((1,), (1,)), ((0,), (0,))),
                                  preferred_element_type=F32)
    dec = jnp.exp(b_last)

    s_cur = state_ref[...]
    for n in range(nb):
        sprev_ref[n] = s_cur.astype(BF16)
        s_cur = s_cur * dec[n] + kv_ref[n]
    state_ref[...] = s_cur

    qd = (q3 * jnp.exp(b3)).astype(BF16)
    o = lax.dot_general(qd, sprev_ref[...], (((2,), (2,)), ((0,), (0,))),
                        preferred_element_type=F32)

    row = lax.broadcasted_iota(jnp.int32, (nb, c, 1), 1)
    for s in range(c):
        live = row >= s
        diff = b3 - b3[:, s:s + 1, :]
        w = jnp.exp(jnp.where(live, diff, 0.0))
        col = jnp.sum(q3 * w * k3[:, s:s + 1, :], axis=-1, keepdims=True)
        o = o + jnp.where(live, col, 0.0) * v3[:, s:s + 1, :]

    o = o.reshape(tb, HG_DK)
    g = g_ref[...]
    o_ref[...] = (o * _rms_scale(o) * gain_ref[...] * (g * _sigmoid(g))).astype(o_ref.dtype)


def _hgrn(proj, lb_logits, gain, layer_j, *, tb=1024):
    s = proj.shape[0]
    heads = gain.shape[0] // HG_DK
    n_even = lb_logits.shape[0]
    nb = tb // HG_SUB
    blk = lambda off: pl.BlockSpec((tb, HG_DK), lambda h, t, off=off: (t, off + h))
    return pl.pallas_call(
        functools.partial(_hgrn_kernel, layer_j=layer_j, tb=tb),
        grid=(heads, s // tb),
        in_specs=[
            blk(0), blk(heads), blk(2 * heads), blk(3 * heads),
            pl.BlockSpec((n_even, HG_DK), lambda h, t: (0, h)),
            pl.BlockSpec((1, HG_DK), lambda h, t: (0, h)),
        ],
        out_specs=pl.BlockSpec((tb, HG_DK), lambda h, t: (t, h)),
        out_shape=jax.ShapeDtypeStruct((s, heads * HG_DK), BF16),
        scratch_shapes=[
            pltpu.VMEM((HG_DK, HG_DK), F32),
            pltpu.VMEM((nb, HG_DK, HG_DK), F32),
            pltpu.VMEM((nb, HG_DK, HG_DK), BF16),
        ],
        compiler_params=_params("parallel", "arbitrary"),
        name="hgrn2",
    )(proj, proj, proj, proj, lb_logits, gain.reshape(1, -1))


def _split_bf16(x, parts):
    out = []
    for _ in range(parts):
        hi = x.astype(BF16)
        out.append(hi)
        x = x - hi.astype(F32)
    return out


def _ssd_kernel(z_ref, xs_ref, bc_ref, xsh_ref, bch_ref, dt_ref, cwx_ref, cwbc_ref, cbx_ref, cbbc_ref,
                dtb_ref, alog_ref, dskip_ref, gain_ref, expand_ref, o_ref, state_ref):
    L = SSD_CHUNK
    inner = z_ref.shape[1]
    heads = inner // SSD_P
    t = pl.program_id(0)

    @pl.when(t == 0)
    def _():
        state_ref[...] = jnp.zeros_like(state_ref)

    halo_on = (t > 0).astype(F32)

    def conv_silu(x_ref, halo_ref, w_ref, b_ref):
        x = x_ref[...]
        full = jnp.concatenate([halo_ref[...] * halo_on, x], axis=0)
        w = w_ref[...]
        acc = b_ref[...] + w[SSD_CONV - 1:SSD_CONV] * x
        for kk in range(SSD_CONV - 1):
            off = 8 - (SSD_CONV - 1) + kk
            acc = acc + w[kk:kk + 1] * full[off:off + L]
        return acc * _sigmoid(acc)

    xs = conv_silu(xs_ref, xsh_ref, cwx_ref, cbx_ref)
    bc = conv_silu(bc_ref, bch_ref, cwbc_ref, cbbc_ref)
    bm = bc[:, :SSD_G * SSD_N]
    cm = bc[:, SSD_G * SSD_N:]

    pre = dt_ref[...] + dtb_ref[...]
    dt = jnp.maximum(pre, 0.0) + jnp.log(1.0 + jnp.exp(-jnp.abs(pre)))
    a = dt * (-jnp.exp(alog_ref[...]))
    cs = _cumsum_rows(a, L)
    cs_t = cs.T

    expand = expand_ref[...]

    def spread(x, parts):
        y = jnp.zeros((x.shape[0], inner), F32)
        for piece in _split_bf16(x, parts):
            y = y + jnp.dot(piece, expand, preferred_element_type=F32)
        return y

    dt_e = spread(dt, 3)
    cs_e = spread(cs, 3)
    cs_last_e = cs_e[L - 1:L, :]
    xdt = xs * dt_e
    xdt_b = xdt.astype(BF16)
    w_state = (xdt * jnp.exp(cs_last_e - cs_e)).astype(BF16)
    grow = jnp.exp(cs_e)
    dec_e = jnp.exp(cs_last_e)

    tril = (lax.broadcasted_iota(jnp.int32, (L, L), 0) >= lax.broadcasted_iota(jnp.int32, (L, L), 1))
    lane = lax.broadcasted_iota(jnp.int32, (1, 2 * SSD_P), 1)
    rpg = heads // SSD_G
    gw = rpg * SSD_P
    y_parts = []
    for g in range(SSD_G):
        cg = cm[:, g * SSD_N:(g + 1) * SSD_N].astype(BF16)
        bg = bm[:, g * SSD_N:(g + 1) * SSD_N].astype(BF16)
        cb = lax.dot_general(cg, bg, (((1,), (1,)), ((), ())), preferred_element_type=F32)
        st = state_ref[:, g * gw:(g + 1) * gw]
        y_off = jnp.dot(cg, st.astype(BF16), preferred_element_type=F32)
        new = jnp.dot(bg.T, w_state[:, g * gw:(g + 1) * gw], preferred_element_type=F32)
        state_ref[:, g * gw:(g + 1) * gw] = st * dec_e[:, g * gw:(g + 1) * gw] + new
        for pr in range(rpg // 2):
            c0 = g * gw + pr * 2 * SSD_P
            x_pair = xdt_b[:, c0:c0 + 2 * SSD_P]
            y_pair = jnp.zeros((L, 2 * SSD_P), F32)
            for half in range(2):
                h = g * rpg + pr * 2 + half
                diff = cs[:, h:h + 1] - cs_t[h:h + 1, :]
                decay = jnp.where(tril, jnp.exp(jnp.where(tril, diff, 0.0)), 0.0)
                m = (cb * decay).astype(BF16)
                keep = (lane >= SSD_P) if half else (lane < SSD_P)
                x_half = jnp.where(keep, x_pair, jnp.zeros_like(x_pair))
                y_pair = y_pair + jnp.dot(m, x_half, preferred_element_type=F32)
            y_parts.append(y_pair + y_off[:, pr * 2 * SSD_P:(pr + 1) * 2 * SSD_P]
                           * grow[:, c0:c0 + 2 * SSD_P])
    y = jnp.concatenate(y_parts, axis=-1)
    y = y + xs * dskip_ref[...]
    z = z_ref[...]
    y = y * (z * _sigmoid(z))
    o_ref[...] = (y * _rms_scale(y) * gain_ref[...]).astype(o_ref.dtype)


def _ssd(proj, dt_raw, conv_w, conv_b, dt_bias, a_log, d_skip, gain, z_blk, xs_blk):
    s = proj.shape[0]
    inner = gain.shape[0]
    heads = inner // SSD_P
    L = SSD_CHUNK
    assert 2 * SSD_G * SSD_N == inner and heads <= LANES

    def pad_heads(x):
        return jnp.pad(x.astype(F32), (0, LANES - heads)).reshape(1, LANES)

    expand = (jnp.arange(LANES)[:, None] == (jnp.arange(inner)[None, :] // SSD_P)).astype(BF16)
    halo = lambda blk: pl.BlockSpec((8, inner), lambda t, blk=blk: (jnp.maximum(t * (L // 8) - 1, 0), blk))
    main = lambda blk: pl.BlockSpec((L, inner), lambda t, blk=blk: (t, blk))
    row = lambda width: pl.BlockSpec((1, width), lambda t: (0, 0))
    return pl.pallas_call(
        _ssd_kernel,
        grid=(s // L,),
        in_specs=[
            main(z_blk), main(xs_blk), main(xs_blk + 1), halo(xs_blk), halo(xs_blk + 1),
            pl.BlockSpec((L, LANES), lambda t: (t, 0)),
            pl.BlockSpec((SSD_CONV, inner), lambda t: (0, 0)),
            pl.BlockSpec((SSD_CONV, inner), lambda t: (0, 1)),
            pl.BlockSpec((1, inner), lambda t: (0, 0)),
            pl.BlockSpec((1, inner), lambda t: (0, 1)),
            row(LANES), row(LANES), row(inner), row(inner),
            pl.BlockSpec((LANES, inner), lambda t: (0, 0)),
        ],
        out_specs=pl.BlockSpec((L, inner), lambda t: (t, 0)),
        out_shape=jax.ShapeDtypeStruct((s, inner), BF16),
        scratch_shapes=[pltpu.VMEM((SSD_N, inner), F32)],
        compiler_params=_params("arbitrary"),
        name="ssd",
    )(proj, proj, proj, proj, proj, dt_raw, conv_w, conv_w, conv_b.reshape(1, -1), conv_b.reshape(1, -1),
      pad_heads(dt_bias), pad_heads(a_log), jnp.repeat(d_skip.astype(F32), SSD_P).reshape(1, inner),
      gain.reshape(1, inner), expand)


def _logf_cumsum_kernel(f_ref, b_ref, o_ref, carry_ref, *, heads):
    @pl.when(pl.program_id(0) == 0)
    def _():
        carry_ref[...] = jnp.zeros_like(carry_ref)

    x = f_ref[...] + b_ref[...]
    ls = jnp.minimum(x, 0.0) - jnp.log(1.0 + jnp.exp(-jnp.abs(x)))
    cum = _cumsum_rows(ls, x.shape[0]) + carry_ref[...]
    carry_ref[...] = cum[x.shape[0] - 1:, :]
    lane = lax.broadcasted_iota(jnp.int32, cum.shape, 1)
    out = jnp.zeros_like(cum)
    for p, piece in enumerate(_split_bf16(cum, CUM_PIECES)):
        piece = jnp.where(lane < heads, piece.astype(F32), 0.0)
        out = out + (pltpu.roll(piece, p * heads, 1) if p else piece)
    o_ref[...] = out.astype(BF16)


def _logf_cumsum(f_logit, b_f, heads, *, tb=2048):
    s = f_logit.shape[0]
    assert CUM_PIECES * heads <= LANES
    return pl.pallas_call(
        functools.partial(_logf_cumsum_kernel, heads=heads),
        grid=(s // tb,),
        in_specs=[pl.BlockSpec((tb, LANES), lambda t: (t, 0)), pl.BlockSpec((1, LANES), lambda t: (0, 0))],
        out_specs=pl.BlockSpec((tb, LANES), lambda t: (t, 0)),
        out_shape=jax.ShapeDtypeStruct((s, LANES), BF16),
        scratch_shapes=[pltpu.VMEM((1, LANES), F32)],
        compiler_params=_params("arbitrary"),
        name="logf_cumsum",
    )(f_logit, b_f)


def _attn_kernel(q_ref, k_ref, v_ref, c_ref, selq_ref, selk_ref, o_ref,
                 kaug_ref, vaug_ref, m_ref, acc_ref, first_ref, *, tq, tk, unroll):
    i = pl.program_id(1)
    lane = lax.broadcasted_iota(jnp.int32, (1, LANES), 1)
    c = c_ref[...]
    own = pl.ds(pl.multiple_of(i * tq, tq), tq)
    past = i * (tq // tk)

    def one_head(side, carry):
        data = (lane >= side * FOX_HD) & (lane < (side + 1) * FOX_HD)
        spare = FOX_HD * (1 - side)
        in_span = lambda lo, n: ((lane >= lo) & (lane < lo + n)).astype(F32)
        ones_k = in_span(spare, CUM_PIECES)
        ones_q = in_span(spare + CUM_PIECES, CUM_PIECES)
        ones_v = in_span(spare, 1)

        k_bias = jnp.dot(c, selk_ref[0, side], preferred_element_type=F32)
        kaug_ref[side, own, :] = (jnp.where(data, k_ref[...].astype(F32), 0.0) + k_bias + ones_k).astype(BF16)
        vaug_ref[side, own, :] = (jnp.where(data, v_ref[...].astype(F32), 0.0) + ones_v).astype(BF16)
        q_bias = jnp.dot(c, selq_ref[0, side], preferred_element_type=F32)
        q_aug = (jnp.where(data, q_ref[...].astype(F32), 0.0) + q_bias + ones_q).astype(BF16)
        m_ref[...] = jnp.full_like(m_ref, NEG_INF)
        acc_ref[...] = jnp.zeros_like(acc_ref)

        def scores(j, row0):
            start = pl.multiple_of(j * tk, tk)
            return lax.dot_general(q_aug[row0:, :], kaug_ref[side, pl.ds(start, tk), :],
                                   (((1,), (1,)), ((), ())), preferred_element_type=F32)

        def absorb(s, j, row0, masked):
            n = tq - row0
            if masked:
                qpos = lax.broadcasted_iota(jnp.int32, (n, tk), 0)
                kpos = lax.broadcasted_iota(jnp.int32, (n, tk), 1)
                s = jnp.where(kpos <= qpos, s, NEG_INF)
            rows = pl.ds(row0, n)
            v = vaug_ref[side, pl.ds(pl.multiple_of(j * tk, tk), tk), :]
            m_old = m_ref[rows, :]
            m_new = jnp.maximum(m_old, jnp.max(s, axis=-1, keepdims=True))
            p = jnp.exp(s - jnp.concatenate([m_new] * (tk // LANES), axis=1)).astype(BF16)
            acc_ref[rows, :] = (jnp.exp(m_old - m_new) * acc_ref[rows, :]
                                + jnp.dot(p, v, preferred_element_type=F32))
            m_ref[rows, :] = m_new

        def body(jj, inner):
            for u in range(unroll):
                j = jj * unroll + u
                absorb(scores(j, 0), j, 0, False)
            return inner

        lax.fori_loop(0, past // unroll, body, 0)
        for rel in range(tq // tk):
            absorb(scores(past + rel, rel * tk), past + rel, rel * tk, True)
        acc = acc_ref[...]
        out = acc / jnp.sum(acc * ones_v, axis=-1, keepdims=True)

        @pl.when(side == 0)
        def _():
            first_ref[...] = out

        @pl.when(side == 1)
        def _():
            o_ref[...] = jnp.where(data, out, first_ref[...]).astype(o_ref.dtype)

        return carry

    lax.fori_loop(0, 2, one_head, 0)


def _attention(qkv, cum, heads, *, tq=4096, tk=512, unroll=2):
    s = qkv.shape[0]
    assert s % tq == 0 and (tq // tk) % unroll == 0 and tq % tk == 0
    assert heads % 2 == 0 and 2 * FOX_HD == LANES
    pairs = heads // 2
    eye = jnp.eye(LANES, dtype=F32)

    def selector(offset, sign):
        rows = []
        for h in range(heads):
            spare = FOX_HD * (1 - h % 2)
            src = eye[jnp.arange(CUM_PIECES) * heads + h]
            dst = eye[spare + offset + jnp.arange(CUM_PIECES)]
            rows.append(sign * jnp.einsum("pa,pb->ab", src, dst))
        return jnp.stack(rows).reshape(pairs, 2, LANES, LANES).astype(BF16)

    selq = selector(0, 1.0)
    selk = selector(CUM_PIECES, -1.0)
    blk = lambda off: pl.BlockSpec((tq, LANES), lambda p, i, off=off: (i, off + p))
    sel = pl.BlockSpec((1, 2, LANES, LANES), lambda p, i: (p, 0, 0, 0))
    return pl.pallas_call(
        functools.partial(_attn_kernel, tq=tq, tk=tk, unroll=unroll),
        grid=(pairs, s // tq),
        in_specs=[blk(0), blk(pairs), blk(2 * pairs),
                  pl.BlockSpec((tq, LANES), lambda p, i: (i, 0)), sel, sel],
        out_specs=pl.BlockSpec((tq, LANES), lambda p, i: (i, p)),
        out_shape=jax.ShapeDtypeStruct((s, heads * FOX_HD), BF16),
        scratch_shapes=[pltpu.VMEM((2, s, LANES), BF16), pltpu.VMEM((2, s, LANES), BF16),
                        pltpu.VMEM((tq, LANES), F32), pltpu.VMEM((tq, LANES), F32),
                        pltpu.VMEM((tq, LANES), F32)],
        compiler_params=_params("arbitrary", "arbitrary"),
        name="fox_attention",
    )(qkv, qkv, qkv, cum, selq, selk)


def _pad_cols(w, width):
    return jnp.pad(w, ((0, 0), (0, width - w.shape[1])))


def kernel(x, ffn1_norm, ffn1_w_gate, ffn1_w_up, ffn1_w_down, mix_norm, hyb_w_in, hgrn_lb_logits,
           hgrn_out_gain, ssd_conv_w, ssd_conv_b, ssd_dt_bias, ssd_A_log, ssd_D, ssd_out_gain, hyb_w_out,
           fox_w_in, fox_b_f, fox_w_out, ffn2_norm, ffn2_w_gate, ffn2_w_up, ffn2_w_down, final_norm):
    bsz, s, d = x.shape
    depth = ffn1_norm.shape[0]
    hg_width = hgrn_out_gain.shape[1]
    inner = ssd_out_gain.shape[1]
    ssd_heads = ssd_dt_bias.shape[1]
    fox_heads = fox_b_f.shape[1]
    fox_width = fox_heads * FOX_HD
    hyb_main = 4 * hg_width + inner + inner + 2 * SSD_G * SSD_N
    assert hg_width == inner == d and hyb_w_in.shape[2] == hyb_main + ssd_heads

    bf = lambda w: w.astype(BF16)
    ffn1 = (bf(ffn1_w_gate), bf(ffn1_w_up), bf(ffn1_w_down))
    ffn2 = (bf(ffn2_w_gate), bf(ffn2_w_up), bf(ffn2_w_down))
    hyb_in, hyb_out = bf(hyb_w_in), bf(hyb_w_out)
    col_scale = jnp.where(jnp.arange(fox_w_in.shape[2]) < fox_width, FOX_HD ** -0.5, 1.0)
    fox_in = bf(fox_w_in * col_scale)
    fox_out = bf(fox_w_out)

    outs = []
    for b in range(bsz):
        h = x[b]
        for layer in range(depth):
            j = layer // 2
            h = _ffn(h, ffn1_norm[layer], *ffn1, layer)
            if layer % 2 == 0:
                proj, dt_raw = _norm_proj(h, mix_norm[layer], hyb_in, j, hyb_main,
                                          _pad_cols(hyb_in[j, :, hyb_main:], LANES), F32)
                o_a = _hgrn(proj, hgrn_lb_logits, hgrn_out_gain[j], j)
                o_b = _ssd(proj, dt_raw, ssd_conv_w[j], ssd_conv_b[j], ssd_dt_bias[j], ssd_A_log[j],
                           ssd_D[j], ssd_out_gain[j], z_blk=4 * hg_width // inner,
                           xs_blk=4 * hg_width // inner + 1)
                h = _out_proj(h, [o_a, o_b], hyb_out, j)
            else:
                qkv, f_logit = _norm_proj(h, mix_norm[layer], fox_in, j, 3 * fox_width,
                                          _pad_cols(fox_in[j, :, 3 * fox_width:], LANES), BF16)
                cum = _logf_cumsum(f_logit, jnp.pad(fox_b_f[j], (0, LANES - fox_heads)).reshape(1, LANES),
                                   fox_heads)
                o = _attention(qkv, cum, fox_heads)
                h = _out_proj(h, [o], fox_out, j)
            last = layer == depth - 1
            h = _ffn(h, ffn2_norm[layer], *ffn2, layer, final_gain=final_norm if last else None)
        outs.append(h)
    return jnp.stack(outs, axis=0)
```

```python
import functools

import jax
import jax.numpy as jnp
from jax import lax
from jax.experimental import pallas as pl
from jax.experimental.pallas import tpu as pltpu

F32 = jnp.float32
BF16 = jnp.bfloat16

EPS = 1e-6
TINY = 1e-20
NEG_INF = -1e30

LANES = 128
VMEM_LIMIT = 56 * 1024 * 1024

HG_DK = 128
HG_SUB = 16
SSD_P = 64
SSD_N = 128
SSD_G = 4
SSD_CHUNK = 128
SSD_CONV = 4
FOX_HD = 64
CUM_PIECES = 3


def _params(*sem):
    return pltpu.CompilerParams(dimension_semantics=sem, vmem_limit_bytes=VMEM_LIMIT)


def _rms_scale(x):
    return lax.rsqrt(jnp.mean(x * x, axis=-1, keepdims=True) + EPS)


def _sigmoid(x):
    return 1.0 / (1.0 + jnp.exp(-x))


def _cumsum_rows(x, period):
    row = lax.broadcasted_iota(jnp.int32, x.shape, 0) % period
    shift = 1
    while shift < period:
        x = x + jnp.where(row >= shift, pltpu.roll(x, shift, 0), 0.0)
        shift *= 2
    return x


def _ffn_kernel(h_ref, g_ref, wg_ref, wu_ref, wd_ref, fg_ref, o_ref, u_ref, acc_ref, *, tf, final_norm):
    x = h_ref[...]
    u_ref[...] = (x * _rms_scale(x) * g_ref[...]).astype(BF16)
    ff = wg_ref.shape[1]
    for c in range(ff // tf):
        u = u_ref[...]
        cols = slice(c * tf, (c + 1) * tf)
        gate = jnp.dot(u, wg_ref[:, cols], preferred_element_type=F32)
        up = jnp.dot(u, wu_ref[:, cols], preferred_element_type=F32)
        act = (gate * _sigmoid(gate) * up).astype(BF16)
        part = jnp.dot(act, wd_ref[cols, :], preferred_element_type=F32)
        if c == 0:
            acc_ref[...] = part
        else:
            acc_ref[...] += part
    y = h_ref[...] + 0.5 * acc_ref[...]
    if final_norm:
        y = y * _rms_scale(y) * fg_ref[...]
    o_ref[...] = y


def _ffn(h, gain, wg, wu, wd, layer, final_gain=None, *, tm=1024, tf=256):
    s, d = h.shape
    ff = wg.shape[2]
    assert s % tm == 0 and ff % tf == 0
    final_norm = final_gain is not None
    fg = final_gain if final_norm else gain
    resident = lambda shape: pl.BlockSpec((None,) + shape, lambda i: (layer, 0, 0),
                                          pipeline_mode=pl.Buffered(1))
    return pl.pallas_call(
        functools.partial(_ffn_kernel, tf=tf, final_norm=final_norm),
        grid=(s // tm,),
        in_specs=[
            pl.BlockSpec((tm, d), lambda i: (i, 0)),
            pl.BlockSpec((1, d), lambda i: (0, 0)),
            resident((d, ff)), resident((d, ff)), resident((ff, d)),
            pl.BlockSpec((1, d), lambda i: (0, 0)),
        ],
        out_specs=pl.BlockSpec((tm, d), lambda i: (i, 0)),
        out_shape=jax.ShapeDtypeStruct((s, d), F32),
        scratch_shapes=[pltpu.VMEM((tm, d), BF16), pltpu.VMEM((tm, d), F32)],
        compiler_params=_params("parallel"),
        name="ffn",
    )(h, gain.reshape(1, d), wg, wu, wd, fg.reshape(1, d))


def _proj_kernel(h_ref, g_ref, w_ref, ws_ref, o_ref, os_ref, u_ref, *, tn):
    x = h_ref[...]
    u_ref[...] = (x * _rms_scale(x) * g_ref[...]).astype(BF16)
    os_ref[...] = jnp.dot(u_ref[...], ws_ref[...], preferred_element_type=F32)
    for c in range(o_ref.shape[1] // tn):
        cols = slice(c * tn, (c + 1) * tn)
        o_ref[:, cols] = jnp.dot(u_ref[...], w_ref[:, cols], preferred_element_type=F32).astype(o_ref.dtype)


def _norm_proj(h, gain, w_main, layer, n, w_small, out_dtype, *, tm=512, tn=1024):
    s, d = h.shape
    assert s % tm == 0 and n % tn == 0 and n <= w_main.shape[2]
    return pl.pallas_call(
        functools.partial(_proj_kernel, tn=tn),
        grid=(s // tm,),
        in_specs=[
            pl.BlockSpec((tm, d), lambda i: (i, 0)),
            pl.BlockSpec((1, d), lambda i: (0, 0)),
            pl.BlockSpec((None, d, n), lambda i: (layer, 0, 0), pipeline_mode=pl.Buffered(1)),
            pl.BlockSpec((d, LANES), lambda i: (0, 0)),
        ],
        out_specs=[
            pl.BlockSpec((tm, n), lambda i: (i, 0)),
            pl.BlockSpec((tm, LANES), lambda i: (i, 0)),
        ],
        out_shape=[jax.ShapeDtypeStruct((s, n), out_dtype), jax.ShapeDtypeStruct((s, LANES), F32)],
        scratch_shapes=[pltpu.VMEM((tm, d), BF16)],
        compiler_params=_params("parallel"),
        name="norm_proj",
    )(h, gain.reshape(1, d), w_main, w_small)


def _out_proj_kernel(*refs, n_in):
    h_ref = refs[0]
    x_refs = refs[1:1 + n_in]
    w_refs = refs[1 + n_in:1 + 2 * n_in]
    o_ref = refs[1 + 2 * n_in]
    y = h_ref[...]
    for x_ref, w_ref in zip(x_refs, w_refs):
        y = y + jnp.dot(x_ref[...], w_ref[...], preferred_element_type=F32)
    o_ref[...] = y


def _out_proj(h, xs, w, layer, *, tm=1024):
    s, d = h.shape
    n_in = len(xs)
    in_specs = [pl.BlockSpec((tm, d), lambda i: (i, 0))]
    in_specs += [pl.BlockSpec((tm, x.shape[1]), lambda i: (i, 0)) for x in xs]
    row = 0
    for x in xs:
        width = x.shape[1]
        assert row % width == 0
        in_specs.append(pl.BlockSpec((None, width, d), lambda i, blk=row // width: (layer, blk, 0)))
        row += width
    assert row == w.shape[1]
    return pl.pallas_call(
        functools.partial(_out_proj_kernel, n_in=n_in),
        grid=(s // tm,),
        in_specs=in_specs,
        out_specs=pl.BlockSpec((tm, d), lambda i: (i, 0)),
        out_shape=jax.ShapeDtypeStruct((s, d), F32),
        compiler_params=_params("parallel"),
        name="out_proj",
    )(h, *xs, *([w] * n_in))


def _hgrn_kernel(q_ref, f_ref, v_ref, g_ref, lbl_ref, gain_ref, o_ref,
                 state_ref, kv_ref, sprev_ref, *, layer_j, tb):
    c = HG_SUB
    nb = tb // c

    @pl.when(pl.program_id(1) == 0)
    def _():
        state_ref[...] = jnp.zeros_like(state_ref)

    logits = lbl_ref[...]
    e = jnp.exp(logits - jnp.max(logits, axis=0, keepdims=True))
    p = e / jnp.sum(e, axis=0, keepdims=True)
    lb = jnp.zeros((1, HG_DK), F32)
    for i in range(1, layer_j + 1):
        lb = lb + p[i:i + 1]

    fl = f_ref[...]
    f = lb + (1.0 - lb) * _sigmoid(fl)
    logf = jnp.log(jnp.maximum(f, TINY))
    k = (1.0 - lb) * _sigmoid(-fl)
    q = q_ref[...]
    v = v_ref[...]

    b = _cumsum_rows(logf, c)
    b3 = b.reshape(nb, c, HG_DK)
    q3 = q.reshape(nb, c, HG_DK)
    k3 = k.reshape(nb, c, HG_DK)
    v3 = v.reshape(nb, c, HG_DK)
    b_last = b3[:, c - 1:c, :]

    kd = (k3 * jnp.exp(b_last - b3)).astype(BF16)
    kv_ref[...] = lax.dot_general(v3.astype(BF16), kd, (((1,), (1,)), ((0,), (0,))),
                                  preferred_element_type=F32)
    dec = jnp.exp(b_last)

    s_cur = state_ref[...]
    for n in range(nb):
        sprev_ref[n] = s_cur.astype(BF16)
        s_cur = s_cur * dec[n] + kv_ref[n]
    state_ref[...] = s_cur

    qd = (q3 * jnp.exp(b3)).astype(BF16)
    o = lax.dot_general(qd, sprev_ref[...], (((2,), (2,)), ((0,), (0,))),
                        preferred_element_type=F32)

    row = lax.broadcasted_iota(jnp.int32, (nb, c, 1), 1)
    for s in range(c):
        live = row >= s
        diff = b3 - b3[:, s:s + 1, :]
        w = jnp.exp(jnp.where(live, diff, 0.0))
        col = jnp.sum(q3 * w * k3[:, s:s + 1, :], axis=-1, keepdims=True)
        o = o + jnp.where(live, col, 0.0) * v3[:, s:s + 1, :]

    o = o.reshape(tb, HG_DK)
    g = g_ref[...]
    o_ref[...] = (o * _rms_scale(o) * gain_ref[...] * (g * _sigmoid(g))).astype(o_ref.dtype)


def _hgrn(proj, lb_logits, gain, layer_j, *, tb=2048):
    s = proj.shape[0]
    heads = gain.shape[0] // HG_DK
    n_even = lb_logits.shape[0]
    nb = tb // HG_SUB
    blk = lambda off: pl.BlockSpec((tb, HG_DK), lambda h, t, off=off: (t, off + h))
    return pl.pallas_call(
        functools.partial(_hgrn_kernel, layer_j=layer_j, tb=tb),
        grid=(heads, s // tb),
        in_specs=[
            blk(0), blk(heads), blk(2 * heads), blk(3 * heads),
            pl.BlockSpec((n_even, HG_DK), lambda h, t: (0, h)),
            pl.BlockSpec((1, HG_DK), lambda h, t: (0, h)),
        ],
        out_specs=pl.BlockSpec((tb, HG_DK), lambda h, t: (t, h)),
        out_shape=jax.ShapeDtypeStruct((s, heads * HG_DK), BF16),
        scratch_shapes=[
            pltpu.VMEM((HG_DK, HG_DK), F32),
            pltpu.VMEM((nb, HG_DK, HG_DK), F32),
            pltpu.VMEM((nb, HG_DK, HG_DK), BF16),
        ],
        compiler_params=_params("parallel", "arbitrary"),
        name="hgrn2",
    )(proj, proj, proj, proj, lb_logits, gain.reshape(1, -1))


def _split_bf16(x, parts):
    out = []
    for _ in range(parts):
        hi = x.astype(BF16)
        out.append(hi)
        x = x - hi.astype(F32)
    return out


def _ssd_kernel(z_ref, xs_ref, bc_ref, xsh_ref, bch_ref, dt_ref, cwx_ref, cwbc_ref, cbx_ref, cbbc_ref,
                dtb_ref, alog_ref, dskip_ref, gain_ref, expand_ref, o_ref, state_ref):
    L = SSD_CHUNK
    inner = z_ref.shape[1]
    heads = inner // SSD_P
    t = pl.program_id(0)

    @pl.when(t == 0)
    def _():
        state_ref[...] = jnp.zeros_like(state_ref)

    halo_on = (t > 0).astype(F32)

    def conv_silu(x_ref, halo_ref, w_ref, b_ref):
        x = x_ref[...]
        full = jnp.concatenate([halo_ref[...] * halo_on, x], axis=0)
        w = w_ref[...]
        acc = b_ref[...] + w[SSD_CONV - 1:SSD_CONV] * x
        for kk in range(SSD_CONV - 1):
            off = 8 - (SSD_CONV - 1) + kk
            acc = acc + w[kk:kk + 1] * full[off:off + L]
        return acc * _sigmoid(acc)

    xs = conv_silu(xs_ref, xsh_ref, cwx_ref, cbx_ref)
    bc = conv_silu(bc_ref, bch_ref, cwbc_ref, cbbc_ref)
    bm = bc[:, :SSD_G * SSD_N]
    cm = bc[:, SSD_G * SSD_N:]

    pre = dt_ref[...] + dtb_ref[...]
    dt = jnp.maximum(pre, 0.0) + jnp.log(1.0 + jnp.exp(-jnp.abs(pre)))
    a = dt * (-jnp.exp(alog_ref[...]))
    cs = _cumsum_rows(a, L)
    cs_t = cs.T

    expand = expand_ref[...]

    def spread(x, parts):
        y = jnp.zeros((x.shape[0], inner), F32)
        for piece in _split_bf16(x, parts):
            y = y + jnp.dot(piece, expand, preferred_element_type=F32)
        return y

    dt_e = spread(dt, 3)
    cs_e = spread(cs, 3)
    cs_last_e = cs_e[L - 1:L, :]
    xdt = xs * dt_e
    xdt_b = xdt.astype(BF16)
    w_state = (xdt * jnp.exp(cs_last_e - cs_e)).astype(BF16)
    grow = jnp.exp(cs_e)
    dec_e = jnp.exp(cs_last_e)

    tril = (lax.broadcasted_iota(jnp.int32, (L, L), 0) >= lax.broadcasted_iota(jnp.int32, (L, L), 1))
    lane = lax.broadcasted_iota(jnp.int32, (1, 2 * SSD_P), 1)
    rpg = heads // SSD_G
    gw = rpg * SSD_P
    y_parts = []
    for g in range(SSD_G):
        cg = cm[:, g * SSD_N:(g + 1) * SSD_N].astype(BF16)
        bg = bm[:, g * SSD_N:(g + 1) * SSD_N].astype(BF16)
        cb = lax.dot_general(cg, bg, (((1,), (1,)), ((), ())), preferred_element_type=F32)
        st = state_ref[:, g * gw:(g + 1) * gw]
        y_off = jnp.dot(cg, st.astype(BF16), preferred_element_type=F32)
        new = jnp.dot(bg.T, w_state[:, g * gw:(g + 1) * gw], preferred_element_type=F32)
        state_ref[:, g * gw:(g + 1) * gw] = st * dec_e[:, g * gw:(g + 1) * gw] + new
        for pr in range(rpg // 2):
            c0 = g * gw + pr * 2 * SSD_P
            x_pair = xdt_b[:, c0:c0 + 2 * SSD_P]
            y_pair = jnp.zeros((L, 2 * SSD_P), F32)
            for half in range(2):
                h = g * rpg + pr * 2 + half
                diff = cs[:, h:h + 1] - cs_t[h:h + 1, :]
                decay = jnp.where(tril, jnp.exp(jnp.where(tril, diff, 0.0)), 0.0)
                m = (cb * decay).astype(BF16)
                keep = (lane >= SSD_P) if half else (lane < SSD_P)
                x_half = jnp.where(keep, x_pair, jnp.zeros_like(x_pair))
                y_pair = y_pair + jnp.dot(m, x_half, preferred_element_type=F32)
            y_parts.append(y_pair + y_off[:, pr * 2 * SSD_P:(pr + 1) * 2 * SSD_P]
                           * grow[:, c0:c0 + 2 * SSD_P])
    y = jnp.concatenate(y_parts, axis=-1)
    y = y + xs * dskip_ref[...]
    z = z_ref[...]
    y = y * (z * _sigmoid(z))
    o_ref[...] = (y * _rms_scale(y) * gain_ref[...]).astype(o_ref.dtype)


def _ssd(proj, dt_raw, conv_w, conv_b, dt_bias, a_log, d_skip, gain, z_blk, xs_blk):
    s = proj.shape[0]
    inner = gain.shape[0]
    heads = inner // SSD_P
    L = SSD_CHUNK
    assert 2 * SSD_G * SSD_N == inner and heads <= LANES

    def pad_heads(x):
        return jnp.pad(x.astype(F32), (0, LANES - heads)).reshape(1, LANES)

    expand = (jnp.arange(LANES)[:, None] == (jnp.arange(inner)[None, :] // SSD_P)).astype(BF16)
    halo = lambda blk: pl.BlockSpec((8, inner), lambda t, blk=blk: (jnp.maximum(t * (L // 8) - 1, 0), blk))
    main = lambda blk: pl.BlockSpec((L, inner), lambda t, blk=blk: (t, blk))
    row = lambda width: pl.BlockSpec((1, width), lambda t: (0, 0))
    return pl.pallas_call(
        _ssd_kernel,
        grid=(s // L,),
        in_specs=[
            main(z_blk), main(xs_blk), main(xs_blk + 1), halo(xs_blk), halo(xs_blk + 1),
            pl.BlockSpec((L, LANES), lambda t: (t, 0)),
            pl.BlockSpec((SSD_CONV, inner), lambda t: (0, 0)),
            pl.BlockSpec((SSD_CONV, inner), lambda t: (0, 1)),
            pl.BlockSpec((1, inner), lambda t: (0, 0)),
            pl.BlockSpec((1, inner), lambda t: (0, 1)),
            row(LANES), row(LANES), row(inner), row(inner),
            pl.BlockSpec((LANES, inner), lambda t: (0, 0)),
        ],
        out_specs=pl.BlockSpec((L, inner), lambda t: (t, 0)),
        out_shape=jax.ShapeDtypeStruct((s, inner), BF16),
        scratch_shapes=[pltpu.VMEM((SSD_N, inner), F32)],
        compiler_params=_params("arbitrary"),
        name="ssd",
    )(proj, proj, proj, proj, proj, dt_raw, conv_w, conv_w, conv_b.reshape(1, -1), conv_b.reshape(1, -1),
      pad_heads(dt_bias), pad_heads(a_log), jnp.repeat(d_skip.astype(F32), SSD_P).reshape(1, inner),
      gain.reshape(1, inner), expand)


def _logf_cumsum_kernel(f_ref, b_ref, o_ref, carry_ref, *, heads):
    @pl.when(pl.program_id(0) == 0)
    def _():
        carry_ref[...] = jnp.zeros_like(carry_ref)

    x = f_ref[...] + b_ref[...]
    ls = jnp.minimum(x, 0.0) - jnp.log(1.0 + jnp.exp(-jnp.abs(x)))
    cum = _cumsum_rows(ls, x.shape[0]) + carry_ref[...]
    carry_ref[...] = cum[x.shape[0] - 1:, :]
    lane = lax.broadcasted_iota(jnp.int32, cum.shape, 1)
    out = jnp.zeros_like(cum)
    for p, piece in enumerate(_split_bf16(cum, CUM_PIECES)):
        piece = jnp.where(lane < heads, piece.astype(F32), 0.0)
        out = out + (pltpu.roll(piece, p * heads, 1) if p else piece)
    o_ref[...] = out.astype(BF16)


def _logf_cumsum(f_logit, b_f, heads, *, tb=2048):
    s = f_logit.shape[0]
    assert CUM_PIECES * heads <= LANES
    return pl.pallas_call(
        functools.partial(_logf_cumsum_kernel, heads=heads),
        grid=(s // tb,),
        in_specs=[pl.BlockSpec((tb, LANES), lambda t: (t, 0)), pl.BlockSpec((1, LANES), lambda t: (0, 0))],
        out_specs=pl.BlockSpec((tb, LANES), lambda t: (t, 0)),
        out_shape=jax.ShapeDtypeStruct((s, LANES), BF16),
        scratch_shapes=[pltpu.VMEM((1, LANES), F32)],
        compiler_params=_params("arbitrary"),
        name="logf_cumsum",
    )(f_logit, b_f)


def _attn_kernel(q_ref, k_ref, v_ref, c_ref, selq_ref, selk_ref, o_ref,
                 kaug_ref, vaug_ref, m_ref, acc_ref, first_ref, *, tq, tk, unroll):
    i = pl.program_id(1)
    lane = lax.broadcasted_iota(jnp.int32, (1, LANES), 1)
    c = c_ref[...]
    own = pl.ds(pl.multiple_of(i * tq, tq), tq)
    past = i * (tq // tk)

    def one_head(side, carry):
        data = (lane >= side * FOX_HD) & (lane < (side + 1) * FOX_HD)
        spare = FOX_HD * (1 - side)
        in_span = lambda lo, n: ((lane >= lo) & (lane < lo + n)).astype(F32)
        ones_k = in_span(spare, CUM_PIECES)
        ones_q = in_span(spare + CUM_PIECES, CUM_PIECES)
        ones_v = in_span(spare, 1)

        k_bias = jnp.dot(c, selk_ref[0, side], preferred_element_type=F32)
        kaug_ref[side, own, :] = (jnp.where(data, k_ref[...].astype(F32), 0.0) + k_bias + ones_k).astype(BF16)
        vaug_ref[side, own, :] = (jnp.where(data, v_ref[...].astype(F32), 0.0) + ones_v).astype(BF16)
        q_bias = jnp.dot(c, selq_ref[0, side], preferred_element_type=F32)
        q_aug = (jnp.where(data, q_ref[...].astype(F32), 0.0) + q_bias + ones_q).astype(BF16)
        m_ref[...] = jnp.full_like(m_ref, NEG_INF)
        acc_ref[...] = jnp.zeros_like(acc_ref)

        def scores(j, row0):
            start = pl.multiple_of(j * tk, tk)
            return lax.dot_general(q_aug[row0:, :], kaug_ref[side, pl.ds(start, tk), :],
                                   (((1,), (1,)), ((), ())), preferred_element_type=F32)

        def absorb(s, j, row0, masked):
            n = tq - row0
            if masked:
                qpos = lax.broadcasted_iota(jnp.int32, (n, tk), 0)
                kpos = lax.broadcasted_iota(jnp.int32, (n, tk), 1)
                s = jnp.where(kpos <= qpos, s, NEG_INF)
            rows = pl.ds(row0, n)
            v = vaug_ref[side, pl.ds(pl.multiple_of(j * tk, tk), tk), :]
            m_old = m_ref[rows, :]
            m_new = jnp.maximum(m_old, jnp.max(s, axis=-1, keepdims=True))
            p = jnp.exp(s - jnp.concatenate([m_new] * (tk // LANES), axis=1)).astype(BF16)
            acc_ref[rows, :] = (jnp.exp(m_old - m_new) * acc_ref[rows, :]
                                + jnp.dot(p, v, preferred_element_type=F32))
            m_ref[rows, :] = m_new

        def body(jj, inner):
            for u in range(unroll):
                j = jj * unroll + u
                absorb(scores(j, 0), j, 0, False)
            return inner

        lax.fori_loop(0, past // unroll, body, 0)
        for rel in range(tq // tk):
            absorb(scores(past + rel, rel * tk), past + rel, rel * tk, True)
        acc = acc_ref[...]
        out = acc / jnp.sum(acc * ones_v, axis=-1, keepdims=True)

        @pl.when(side == 0)
        def _():
            first_ref[...] = out

        @pl.when(side == 1)
        def _():
            o_ref[...] = jnp.where(data, out, first_ref[...]).astype(o_ref.dtype)

        return carry

    lax.fori_loop(0, 2, one_head, 0)


def _attention(qkv, cum, heads, *, tq=4096, tk=512, unroll=4):
    s = qkv.shape[0]
    assert s % tq == 0 and (tq // tk) % unroll == 0 and tq % tk == 0
    assert heads % 2 == 0 and 2 * FOX_HD == LANES
    pairs = heads // 2
    eye = jnp.eye(LANES, dtype=F32)

    def selector(offset, sign):
        rows = []
        for h in range(heads):
            spare = FOX_HD * (1 - h % 2)
            src = eye[jnp.arange(CUM_PIECES) * heads + h]
            dst = eye[spare + offset + jnp.arange(CUM_PIECES)]
            rows.append(sign * jnp.einsum("pa,pb->ab", src, dst))
        return jnp.stack(rows).reshape(pairs, 2, LANES, LANES).astype(BF16)

    selq = selector(0, 1.0)
    selk = selector(CUM_PIECES, -1.0)
    blk = lambda off: pl.BlockSpec((tq, LANES), lambda p, i, off=off: (i, off + p))
    sel = pl.BlockSpec((1, 2, LANES, LANES), lambda p, i: (p, 0, 0, 0))
    return pl.pallas_call(
        functools.partial(_attn_kernel, tq=tq, tk=tk, unroll=unroll),
        grid=(pairs, s // tq),
        in_specs=[blk(0), blk(pairs), blk(2 * pairs),
                  pl.BlockSpec((tq, LANES), lambda p, i: (i, 0)), sel, sel],
        out_specs=pl.BlockSpec((tq, LANES), lambda p, i: (i, p)),
        out_shape=jax.ShapeDtypeStruct((s, heads * FOX_HD), BF16),
        scratch_shapes=[pltpu.VMEM((2, s, LANES), BF16), pltpu.VMEM((2, s, LANES), BF16),
                        pltpu.VMEM((tq, LANES), F32), pltpu.VMEM((tq, LANES), F32),
                        pltpu.VMEM((tq, LANES), F32)],
        compiler_params=_params("arbitrary", "arbitrary"),
        name="fox_attention",
    )(qkv, qkv, qkv, cum, selq, selk)


def _pad_cols(w, width):
    return jnp.pad(w, ((0, 0), (0, width - w.shape[1])))


def kernel(x, ffn1_norm, ffn1_w_gate, ffn1_w_up, ffn1_w_down, mix_norm, hyb_w_in, hgrn_lb_logits,
           hgrn_out_gain, ssd_conv_w, ssd_conv_b, ssd_dt_bias, ssd_A_log, ssd_D, ssd_out_gain, hyb_w_out,
           fox_w_in, fox_b_f, fox_w_out, ffn2_norm, ffn2_w_gate, ffn2_w_up, ffn2_w_down, final_norm):
    bsz, s, d = x.shape
    depth = ffn1_norm.shape[0]
    hg_width = hgrn_out_gain.shape[1]
    inner = ssd_out_gain.shape[1]
    ssd_heads = ssd_dt_bias.shape[1]
    fox_heads = fox_b_f.shape[1]
    fox_width = fox_heads * FOX_HD
    hyb_main = 4 * hg_width + inner + inner + 2 * SSD_G * SSD_N
    assert hg_width == inner == d and hyb_w_in.shape[2] == hyb_main + ssd_heads

    bf = lambda w: w.astype(BF16)
    ffn1 = (bf(ffn1_w_gate), bf(ffn1_w_up), bf(ffn1_w_down))
    ffn2 = (bf(ffn2_w_gate), bf(ffn2_w_up), bf(ffn2_w_down))
    hyb_in, hyb_out = bf(hyb_w_in), bf(hyb_w_out)
    col_scale = jnp.where(jnp.arange(fox_w_in.shape[2]) < fox_width, FOX_HD ** -0.5, 1.0)
    fox_in = bf(fox_w_in * col_scale)
    fox_out = bf(fox_w_out)

    outs = []
    for b in range(bsz):
        h = x[b]
        for layer in range(depth):
            j = layer // 2
            h = _ffn(h, ffn1_norm[layer], *ffn1, layer)
            if layer % 2 == 0:
                proj, dt_raw = _norm_proj(h, mix_norm[layer], hyb_in, j, hyb_main,
                                          _pad_cols(hyb_in[j, :, hyb_main:], LANES), F32)
                o_a = _hgrn(proj, hgrn_lb_logits, hgrn_out_gain[j], j)
                o_b = _ssd(proj, dt_raw, ssd_conv_w[j], ssd_conv_b[j], ssd_dt_bias[j], ssd_A_log[j],
                           ssd_D[j], ssd_out_gain[j], z_blk=4 * hg_width // inner,
                           xs_blk=4 * hg_width // inner + 1)
                h = _out_proj(h, [o_a, o_b], hyb_out, j)
            else:
                qkv, f_logit = _norm_proj(h, mix_norm[layer], fox_in, j, 3 * fox_width,
                                          _pad_cols(fox_in[j, :, 3 * fox_width:], LANES), BF16)
                cum = _logf_cumsum(f_logit, jnp.pad(fox_b_f[j], (0, LANES - fox_heads)).reshape(1, LANES),
                                   fox_heads)
                o = _attention(qkv, cum, fox_heads)
                h = _out_proj(h, [o], fox_out, j)
            last = layer == depth - 1
            h = _ffn(h, ffn2_norm[layer], *ffn2, layer, final_gain=final_norm if last else None)
        outs.append(h)
    return jnp.stack(outs, axis=0)
```

```python
import functools

import jax
import jax.numpy as jnp
from jax import lax
from jax.experimental import pallas as pl
from jax.experimental.pallas import tpu as pltpu

F32 = jnp.float32
BF16 = jnp.bfloat16

EPS = 1e-6
TINY = 1e-20
NEG_INF = -1e30

LANES = 128
VMEM_LIMIT = 56 * 1024 * 1024

HG_DK = 128
HG_SUB = 16
SSD_P = 64
SSD_N = 128
SSD_G = 4
SSD_CHUNK = 128
SSD_CONV = 4
FOX_HD = 64
CUM_PIECES = 3


def _params(*sem):
    return pltpu.CompilerParams(dimension_semantics=sem, vmem_limit_bytes=VMEM_LIMIT)


def _rms_scale(x):
    return lax.rsqrt(jnp.mean(x * x, axis=-1, keepdims=True) + EPS)


def _sigmoid(x):
    return 1.0 / (1.0 + jnp.exp(-x))


def _cumsum_rows(x, period):
    row = lax.broadcasted_iota(jnp.int32, x.shape, 0) % period
    shift = 1
    while shift < period:
        x = x + jnp.where(row >= shift, pltpu.roll(x, shift, 0), 0.0)
        shift *= 2
    return x


def _ffn_kernel(h_ref, g_ref, wg_ref, wu_ref, wd_ref, fg_ref, o_ref, u_ref, acc_ref, *, tf, final_norm):
    x = h_ref[...]
    u_ref[...] = (x * _rms_scale(x) * g_ref[...]).astype(BF16)
    ff = wg_ref.shape[1]
    for c in range(ff // tf):
        u = u_ref[...]
        cols = slice(c * tf, (c + 1) * tf)
        gate = jnp.dot(u, wg_ref[:, cols], preferred_element_type=F32)
        up = jnp.dot(u, wu_ref[:, cols], preferred_element_type=F32)
        act = (gate * _sigmoid(gate) * up).astype(BF16)
        part = jnp.dot(act, wd_ref[cols, :], preferred_element_type=F32)
        if c == 0:
            acc_ref[...] = part
        else:
            acc_ref[...] += part
    y = h_ref[...] + 0.5 * acc_ref[...]
    if final_norm:
        y = y * _rms_scale(y) * fg_ref[...]
    o_ref[...] = y


def _ffn(h, gain, wg, wu, wd, layer, final_gain=None, *, tm=1024, tf=256):
    s, d = h.shape
    ff = wg.shape[2]
    assert s % tm == 0 and ff % tf == 0
    final_norm = final_gain is not None
    fg = final_gain if final_norm else gain
    resident = lambda shape: pl.BlockSpec((None,) + shape, lambda i: (layer, 0, 0),
                                          pipeline_mode=pl.Buffered(1))
    return pl.pallas_call(
        functools.partial(_ffn_kernel, tf=tf, final_norm=final_norm),
        grid=(s // tm,),
        in_specs=[
            pl.BlockSpec((tm, d), lambda i: (i, 0)),
            pl.BlockSpec((1, d), lambda i: (0, 0)),
            resident((d, ff)), resident((d, ff)), resident((ff, d)),
            pl.BlockSpec((1, d), lambda i: (0, 0)),
        ],
        out_specs=pl.BlockSpec((tm, d), lambda i: (i, 0)),
        out_shape=jax.ShapeDtypeStruct((s, d), F32),
        scratch_shapes=[pltpu.VMEM((tm, d), BF16), pltpu.VMEM((tm, d), F32)],
        compiler_params=_params("parallel"),
        name="ffn",
    )(h, gain.reshape(1, d), wg, wu, wd, fg.reshape(1, d))


def _proj_kernel(h_ref, g_ref, w_ref, ws_ref, o_ref, os_ref, u_ref, *, tn):
    x = h_ref[...]
    u_ref[...] = (x * _rms_scale(x) * g_ref[...]).astype(BF16)
    os_ref[...] = jnp.dot(u_ref[...], ws_ref[...], preferred_element_type=F32)
    for c in range(o_ref.shape[1] // tn):
        cols = slice(c * tn, (c + 1) * tn)
        o_ref[:, cols] = lax.dot_general(u_ref[...], w_ref[cols, :], (((1,), (1,)), ((), ())),
                                         preferred_element_type=F32).astype(o_ref.dtype)


def _norm_proj(h, gain, w_main, layer, n, w_small, out_dtype, *, tm=512, tn=1024):
    s, d = h.shape
    assert s % tm == 0 and n % tn == 0 and n <= w_main.shape[1]
    return pl.pallas_call(
        functools.partial(_proj_kernel, tn=tn),
        grid=(s // tm,),
        in_specs=[
            pl.BlockSpec((tm, d), lambda i: (i, 0)),
            pl.BlockSpec((1, d), lambda i: (0, 0)),
            pl.BlockSpec((None, n, d), lambda i: (layer, 0, 0), pipeline_mode=pl.Buffered(1)),
            pl.BlockSpec((d, LANES), lambda i: (0, 0)),
        ],
        out_specs=[
            pl.BlockSpec((tm, n), lambda i: (i, 0)),
            pl.BlockSpec((tm, LANES), lambda i: (i, 0)),
        ],
        out_shape=[jax.ShapeDtypeStruct((s, n), out_dtype), jax.ShapeDtypeStruct((s, LANES), F32)],
        scratch_shapes=[pltpu.VMEM((tm, d), BF16)],
        compiler_params=_params("parallel"),
        name="norm_proj",
    )(h, gain.reshape(1, d), w_main, w_small)


def _out_proj_kernel(*refs, n_in):
    h_ref = refs[0]
    x_refs = refs[1:1 + n_in]
    w_refs = refs[1 + n_in:1 + 2 * n_in]
    o_ref = refs[1 + 2 * n_in]
    y = h_ref[...]
    for x_ref, w_ref in zip(x_refs, w_refs):
        y = y + jnp.dot(x_ref[...], w_ref[...], preferred_element_type=F32)
    o_ref[...] = y


def _out_proj(h, xs, w, layer, *, tm=1024):
    s, d = h.shape
    n_in = len(xs)
    in_specs = [pl.BlockSpec((tm, d), lambda i: (i, 0))]
    in_specs += [pl.BlockSpec((tm, x.shape[1]), lambda i: (i, 0)) for x in xs]
    row = 0
    for x in xs:
        width = x.shape[1]
        assert row % width == 0
        in_specs.append(pl.BlockSpec((None, width, d), lambda i, blk=row // width: (layer, blk, 0)))
        row += width
    assert row == w.shape[1]
    return pl.pallas_call(
        functools.partial(_out_proj_kernel, n_in=n_in),
        grid=(s // tm,),
        in_specs=in_specs,
        out_specs=pl.BlockSpec((tm, d), lambda i: (i, 0)),
        out_shape=jax.ShapeDtypeStruct((s, d), F32),
        compiler_params=_params("parallel"),
        name="out_proj",
    )(h, *xs, *([w] * n_in))


def _hgrn_kernel(q_ref, f_ref, v_ref, g_ref, lbl_ref, gain_ref, o_ref,
                 state_ref, kv_ref, sprev_ref, *, layer_j, tb):
    c = HG_SUB
    nb = tb // c

    @pl.when(pl.program_id(1) == 0)
    def _():
        state_ref[...] = jnp.zeros_like(state_ref)

    logits = lbl_ref[...]
    e = jnp.exp(logits - jnp.max(logits, axis=0, keepdims=True))
    p = e / jnp.sum(e, axis=0, keepdims=True)
    lb = jnp.zeros((1, HG_DK), F32)
    for i in range(1, layer_j + 1):
        lb = lb + p[i:i + 1]

    fl = f_ref[...]
    f = lb + (1.0 - lb) * _sigmoid(fl)
    logf = jnp.log(jnp.maximum(f, TINY))
    k = (1.0 - lb) * _sigmoid(-fl)
    q = q_ref[...]
    v = v_ref[...]

    b = _cumsum_rows(logf, c)
    b3 = b.reshape(nb, c, HG_DK)
    q3 = q.reshape(nb, c, HG_DK)
    k3 = k.reshape(nb, c, HG_DK)
    v3 = v.reshape(nb, c, HG_DK)
    b_last = b3[:, c - 1:c, :]

    kd = (k3 * jnp.exp(b_last - b3)).astype(BF16)
    kv_ref[...] = lax.dot_general(v3.astype(BF16), kd, (((1,), (1,)), ((0,), (0,))),
                                  preferred_element_type=F32)
    dec = jnp.exp(b_last)

    s_cur = state_ref[...]
    for n in range(nb):
        sprev_ref[n] = s_cur.astype(BF16)
        s_cur = s_cur * dec[n] + kv_ref[n]
    state_ref[...] = s_cur

    qd = (q3 * jnp.exp(b3)).astype(BF16)
    o = lax.dot_general(qd, sprev_ref[...], (((2,), (2,)), ((0,), (0,))),
                        preferred_element_type=F32)

    row = lax.broadcasted_iota(jnp.int32, (nb, c, 1), 1)
    for s in range(c):
        live = row >= s
        diff = b3 - b3[:, s:s + 1, :]
        w = jnp.exp(jnp.where(live, diff, 0.0))
        col = jnp.sum(q3 * w * k3[:, s:s + 1, :], axis=-1, keepdims=True)
        o = o + jnp.where(live, col, 0.0) * v3[:, s:s + 1, :]

    o = o.reshape(tb, HG_DK)
    g = g_ref[...]
    o_ref[...] = (o * _rms_scale(o) * gain_ref[...] * (g * _sigmoid(g))).astype(o_ref.dtype)


def _hgrn(proj, lb_logits, gain, layer_j, *, tb=2048):
    s = proj.shape[0]
    heads = gain.shape[0] // HG_DK
    n_even = lb_logits.shape[0]
    nb = tb // HG_SUB
    blk = lambda off: pl.BlockSpec((tb, HG_DK), lambda h, t, off=off: (t, off + h))
    return pl.pallas_call(
        functools.partial(_hgrn_kernel, layer_j=layer_j, tb=tb),
        grid=(heads, s // tb),
        in_specs=[
            blk(0), blk(heads), blk(2 * heads), blk(3 * heads),
            pl.BlockSpec((n_even, HG_DK), lambda h, t: (0, h)),
            pl.BlockSpec((1, HG_DK), lambda h, t: (0, h)),
        ],
        out_specs=pl.BlockSpec((tb, HG_DK), lambda h, t: (t, h)),
        out_shape=jax.ShapeDtypeStruct((s, heads * HG_DK), BF16),
        scratch_shapes=[
            pltpu.VMEM((HG_DK, HG_DK), F32),
            pltpu.VMEM((nb, HG_DK, HG_DK), F32),
            pltpu.VMEM((nb, HG_DK, HG_DK), BF16),
        ],
        compiler_params=_params("parallel", "arbitrary"),
        name="hgrn2",
    )(proj, proj, proj, proj, lb_logits, gain.reshape(1, -1))


def _split_bf16(x, parts):
    out = []
    for _ in range(parts):
        hi = x.astype(BF16)
        out.append(hi)
        x = x - hi.astype(F32)
    return out


def _ssd_kernel(z_ref, xs_ref, bc_ref, xsh_ref, bch_ref, dt_ref, cwx_ref, cwbc_ref, cbx_ref, cbbc_ref,
                dtb_ref, alog_ref, dskip_ref, gain_ref, expand_ref, o_ref, state_ref):
    L = SSD_CHUNK
    inner = z_ref.shape[1]
    heads = inner // SSD_P
    t = pl.program_id(0)

    @pl.when(t == 0)
    def _():
        state_ref[...] = jnp.zeros_like(state_ref)

    halo_on = (t > 0).astype(F32)

    def conv_silu(x_ref, halo_ref, w_ref, b_ref):
        x = x_ref[...]
        full = jnp.concatenate([halo_ref[...] * halo_on, x], axis=0)
        w = w_ref[...]
        acc = b_ref[...] + w[SSD_CONV - 1:SSD_CONV] * x
        for kk in range(SSD_CONV - 1):
            off = 8 - (SSD_CONV - 1) + kk
            acc = acc + w[kk:kk + 1] * full[off:off + L]
        return acc * _sigmoid(acc)

    xs = conv_silu(xs_ref, xsh_ref, cwx_ref, cbx_ref)
    bc = conv_silu(bc_ref, bch_ref, cwbc_ref, cbbc_ref)
    bm = bc[:, :SSD_G * SSD_N]
    cm = bc[:, SSD_G * SSD_N:]

    pre = dt_ref[...] + dtb_ref[...]
    dt = jnp.maximum(pre, 0.0) + jnp.log(1.0 + jnp.exp(-jnp.abs(pre)))
    a = dt * (-jnp.exp(alog_ref[...]))
    cs = _cumsum_rows(a, L)
    cs_t = cs.T

    expand = expand_ref[...]

    def spread(x, parts):
        y = jnp.zeros((x.shape[0], inner), F32)
        for piece in _split_bf16(x, parts):
            y = y + jnp.dot(piece, expand, preferred_element_type=F32)
        return y

    dt_e = spread(dt, 3)
    cs_e = spread(cs, 3)
    cs_last_e = cs_e[L - 1:L, :]
    xdt = xs * dt_e
    xdt_b = xdt.astype(BF16)
    w_state = (xdt * jnp.exp(cs_last_e - cs_e)).astype(BF16)
    grow = jnp.exp(cs_e)
    dec_e = jnp.exp(cs_last_e)

    tril = (lax.broadcasted_iota(jnp.int32, (L, L), 0) >= lax.broadcasted_iota(jnp.int32, (L, L), 1))
    lane = lax.broadcasted_iota(jnp.int32, (1, 2 * SSD_P), 1)
    rpg = heads // SSD_G
    gw = rpg * SSD_P
    y_parts = []
    for g in range(SSD_G):
        cg = cm[:, g * SSD_N:(g + 1) * SSD_N].astype(BF16)
        bg = bm[:, g * SSD_N:(g + 1) * SSD_N].astype(BF16)
        cb = lax.dot_general(cg, bg, (((1,), (1,)), ((), ())), preferred_element_type=F32)
        st = state_ref[:, g * gw:(g + 1) * gw]
        y_off = jnp.dot(cg, st.astype(BF16), preferred_element_type=F32)
        new = jnp.dot(bg.T, w_state[:, g * gw:(g + 1) * gw], preferred_element_type=F32)
        state_ref[:, g * gw:(g + 1) * gw] = st * dec_e[:, g * gw:(g + 1) * gw] + new
        for pr in range(rpg // 2):
            c0 = g * gw + pr * 2 * SSD_P
            x_pair = xdt_b[:, c0:c0 + 2 * SSD_P]
            y_pair = jnp.zeros((L, 2 * SSD_P), F32)
            for half in range(2):
                h = g * rpg + pr * 2 + half
                diff = cs[:, h:h + 1] - cs_t[h:h + 1, :]
                decay = jnp.where(tril, jnp.exp(jnp.where(tril, diff, 0.0)), 0.0)
                m = (cb * decay).astype(BF16)
                keep = (lane >= SSD_P) if half else (lane < SSD_P)
                x_half = jnp.where(keep, x_pair, jnp.zeros_like(x_pair))
                y_pair = y_pair + jnp.dot(m, x_half, preferred_element_type=F32)
            y_parts.append(y_pair + y_off[:, pr * 2 * SSD_P:(pr + 1) * 2 * SSD_P]
                           * grow[:, c0:c0 + 2 * SSD_P])
    y = jnp.concatenate(y_parts, axis=-1)
    y = y + xs * dskip_ref[...]
    z = z_ref[...]
    y = y * (z * _sigmoid(z))
    o_ref[...] = (y * _rms_scale(y) * gain_ref[...]).astype(o_ref.dtype)


def _ssd(proj, dt_raw, conv_w, conv_b, dt_bias, a_log, d_skip, gain, z_blk, xs_blk):
    s = proj.shape[0]
    inner = gain.shape[0]
    heads = inner // SSD_P
    L = SSD_CHUNK
    assert 2 * SSD_G * SSD_N == inner and heads <= LANES

    def pad_heads(x):
        return jnp.pad(x.astype(F32), (0, LANES - heads)).reshape(1, LANES)

    expand = (jnp.arange(LANES)[:, None] == (jnp.arange(inner)[None, :] // SSD_P)).astype(BF16)
    halo = lambda blk: pl.BlockSpec((8, inner), lambda t, blk=blk: (jnp.maximum(t * (L // 8) - 1, 0), blk))
    main = lambda blk: pl.BlockSpec((L, inner), lambda t, blk=blk: (t, blk))
    row = lambda width: pl.BlockSpec((1, width), lambda t: (0, 0))
    return pl.pallas_call(
        _ssd_kernel,
        grid=(s // L,),
        in_specs=[
            main(z_blk), main(xs_blk), main(xs_blk + 1), halo(xs_blk), halo(xs_blk + 1),
            pl.BlockSpec((L, LANES), lambda t: (t, 0)),
            pl.BlockSpec((SSD_CONV, inner), lambda t: (0, 0)),
            pl.BlockSpec((SSD_CONV, inner), lambda t: (0, 1)),
            pl.BlockSpec((1, inner), lambda t: (0, 0)),
            pl.BlockSpec((1, inner), lambda t: (0, 1)),
            row(LANES), row(LANES), row(inner), row(inner),
            pl.BlockSpec((LANES, inner), lambda t: (0, 0)),
        ],
        out_specs=pl.BlockSpec((L, inner), lambda t: (t, 0)),
        out_shape=jax.ShapeDtypeStruct((s, inner), BF16),
        scratch_shapes=[pltpu.VMEM((SSD_N, inner), F32)],
        compiler_params=_params("arbitrary"),
        name="ssd",
    )(proj, proj, proj, proj, proj, dt_raw, conv_w, conv_w, conv_b.reshape(1, -1), conv_b.reshape(1, -1),
      pad_heads(dt_bias), pad_heads(a_log), jnp.repeat(d_skip.astype(F32), SSD_P).reshape(1, inner),
      gain.reshape(1, inner), expand)


def _logf_cumsum_kernel(f_ref, b_ref, o_ref, carry_ref, *, heads):
    @pl.when(pl.program_id(0) == 0)
    def _():
        carry_ref[...] = jnp.zeros_like(carry_ref)

    x = f_ref[...] + b_ref[...]
    ls = jnp.minimum(x, 0.0) - jnp.log(1.0 + jnp.exp(-jnp.abs(x)))
    cum = _cumsum_rows(ls, x.shape[0]) + carry_ref[...]
    carry_ref[...] = cum[x.shape[0] - 1:, :]
    lane = lax.broadcasted_iota(jnp.int32, cum.shape, 1)
    out = jnp.zeros_like(cum)
    for p, piece in enumerate(_split_bf16(cum, CUM_PIECES)):
        piece = jnp.where(lane < heads, piece.astype(F32), 0.0)
        out = out + (pltpu.roll(piece, p * heads, 1) if p else piece)
    o_ref[...] = out.astype(BF16)


def _logf_cumsum(f_logit, b_f, heads, *, tb=2048):
    s = f_logit.shape[0]
    assert CUM_PIECES * heads <= LANES
    return pl.pallas_call(
        functools.partial(_logf_cumsum_kernel, heads=heads),
        grid=(s // tb,),
        in_specs=[pl.BlockSpec((tb, LANES), lambda t: (t, 0)), pl.BlockSpec((1, LANES), lambda t: (0, 0))],
        out_specs=pl.BlockSpec((tb, LANES), lambda t: (t, 0)),
        out_shape=jax.ShapeDtypeStruct((s, LANES), BF16),
        scratch_shapes=[pltpu.VMEM((1, LANES), F32)],
        compiler_params=_params("arbitrary"),
        name="logf_cumsum",
    )(f_logit, b_f)


def _attn_kernel(q_ref, k_ref, v_ref, c_ref, selq_ref, selk_ref, o_ref,
                 kaug_ref, vaug_ref, m_ref, acc_ref, first_ref, *, tq, tk, unroll):
    i = pl.program_id(1)
    lane = lax.broadcasted_iota(jnp.int32, (1, LANES), 1)
    c = c_ref[...]
    own = pl.ds(pl.multiple_of(i * tq, tq), tq)
    past = i * (tq // tk)

    def one_head(side, carry):
        data = (lane >= side * FOX_HD) & (lane < (side + 1) * FOX_HD)
        spare = FOX_HD * (1 - side)
        in_span = lambda lo, n: ((lane >= lo) & (lane < lo + n)).astype(F32)
        ones_k = in_span(spare, CUM_PIECES)
        ones_q = in_span(spare + CUM_PIECES, CUM_PIECES)
        ones_v = in_span(spare, 1)

        k_bias = jnp.dot(c, selk_ref[0, side], preferred_element_type=F32)
        kaug_ref[side, own, :] = (jnp.where(data, k_ref[...].astype(F32), 0.0) + k_bias + ones_k).astype(BF16)
        vaug_ref[side, own, :] = (jnp.where(data, v_ref[...].astype(F32), 0.0) + ones_v).astype(BF16)
        q_bias = jnp.dot(c, selq_ref[0, side], preferred_element_type=F32)
        q_aug = (jnp.where(data, q_ref[...].astype(F32), 0.0) + q_bias + ones_q).astype(BF16)
        m_ref[...] = jnp.full_like(m_ref, NEG_INF)
        acc_ref[...] = jnp.zeros_like(acc_ref)

        def scores(j, row0):
            start = pl.multiple_of(j * tk, tk)
            return lax.dot_general(q_aug[row0:, :], kaug_ref[side, pl.ds(start, tk), :],
                                   (((1,), (1,)), ((), ())), preferred_element_type=F32)

        def absorb(s, j, row0, masked):
            n = tq - row0
            if masked:
                qpos = lax.broadcasted_iota(jnp.int32, (n, tk), 0)
                kpos = lax.broadcasted_iota(jnp.int32, (n, tk), 1)
                s = jnp.where(kpos <= qpos, s, NEG_INF)
            rows = pl.ds(row0, n)
            v = vaug_ref[side, pl.ds(pl.multiple_of(j * tk, tk), tk), :]
            m_old = m_ref[rows, :]
            m_new = jnp.maximum(m_old, jnp.max(s, axis=-1, keepdims=True))
            p = jnp.exp(s - jnp.concatenate([m_new] * (tk // LANES), axis=1)).astype(BF16)
            acc_ref[rows, :] = (jnp.exp(m_old - m_new) * acc_ref[rows, :]
                                + jnp.dot(p, v, preferred_element_type=F32))
            m_ref[rows, :] = m_new

        def body(jj, inner):
            for u in range(unroll):
                j = jj * unroll + u
                absorb(scores(j, 0), j, 0, False)
            return inner

        lax.fori_loop(0, past // unroll, body, 0)
        for rel in range(tq // tk):
            absorb(scores(past + rel, rel * tk), past + rel, rel * tk, True)
        acc = acc_ref[...]
        out = acc / jnp.sum(acc * ones_v, axis=-1, keepdims=True)

        @pl.when(side == 0)
        def _():
            first_ref[...] = out

        @pl.when(side == 1)
        def _():
            o_ref[...] = jnp.where(data, out, first_ref[...]).astype(o_ref.dtype)

        return carry

    lax.fori_loop(0, 2, one_head, 0)


def _attention(qkv, cum, heads, *, tq=4096, tk=512, unroll=4):
    s = qkv.shape[0]
    assert s % tq == 0 and (tq // tk) % unroll == 0 and tq % tk == 0
    assert heads % 2 == 0 and 2 * FOX_HD == LANES
    pairs = heads // 2
    eye = jnp.eye(LANES, dtype=F32)

    def selector(offset, sign):
        rows = []
        for h in range(heads):
            spare = FOX_HD * (1 - h % 2)
            src = eye[jnp.arange(CUM_PIECES) * heads + h]
            dst = eye[spare + offset + jnp.arange(CUM_PIECES)]
            rows.append(sign * jnp.einsum("pa,pb->ab", src, dst))
        return jnp.stack(rows).reshape(pairs, 2, LANES, LANES).astype(BF16)

    selq = selector(0, 1.0)
    selk = selector(CUM_PIECES, -1.0)
    blk = lambda off: pl.BlockSpec((tq, LANES), lambda p, i, off=off: (i, off + p))
    sel = pl.BlockSpec((1, 2, LANES, LANES), lambda p, i: (p, 0, 0, 0))
    return pl.pallas_call(
        functools.partial(_attn_kernel, tq=tq, tk=tk, unroll=unroll),
        grid=(pairs, s // tq),
        in_specs=[blk(0), blk(pairs), blk(2 * pairs),
                  pl.BlockSpec((tq, LANES), lambda p, i: (i, 0)), sel, sel],
        out_specs=pl.BlockSpec((tq, LANES), lambda p, i: (i, p)),
        out_shape=jax.ShapeDtypeStruct((s, heads * FOX_HD), BF16),
        scratch_shapes=[pltpu.VMEM((2, s, LANES), BF16), pltpu.VMEM((2, s, LANES), BF16),
                        pltpu.VMEM((tq, LANES), F32), pltpu.VMEM((tq, LANES), F32),
                        pltpu.VMEM((tq, LANES), F32)],
        compiler_params=_params("arbitrary", "arbitrary"),
        name="fox_attention",
    )(qkv, qkv, qkv, cum, selq, selk)


def _pad_cols(w, width):
    return jnp.pad(w, ((0, 0), (0, width - w.shape[1])))


def kernel(x, ffn1_norm, ffn1_w_gate, ffn1_w_up, ffn1_w_down, mix_norm, hyb_w_in, hgrn_lb_logits,
           hgrn_out_gain, ssd_conv_w, ssd_conv_b, ssd_dt_bias, ssd_A_log, ssd_D, ssd_out_gain, hyb_w_out,
           fox_w_in, fox_b_f, fox_w_out, ffn2_norm, ffn2_w_gate, ffn2_w_up, ffn2_w_down, final_norm):
    bsz, s, d = x.shape
    depth = ffn1_norm.shape[0]
    hg_width = hgrn_out_gain.shape[1]
    inner = ssd_out_gain.shape[1]
    ssd_heads = ssd_dt_bias.shape[1]
    fox_heads = fox_b_f.shape[1]
    fox_width = fox_heads * FOX_HD
    hyb_main = 4 * hg_width + inner + inner + 2 * SSD_G * SSD_N
    assert hg_width == inner == d and hyb_w_in.shape[2] == hyb_main + ssd_heads

    bf = lambda w: w.astype(BF16)
    ffn1 = (bf(ffn1_w_gate), bf(ffn1_w_up), bf(ffn1_w_down))
    ffn2 = (bf(ffn2_w_gate), bf(ffn2_w_up), bf(ffn2_w_down))
    hyb_in, hyb_out = bf(hyb_w_in.transpose(0, 2, 1)), bf(hyb_w_out)
    row_scale = jnp.where(jnp.arange(fox_w_in.shape[2]) < fox_width, FOX_HD ** -0.5, 1.0)[:, None]
    fox_in = bf(fox_w_in.transpose(0, 2, 1) * row_scale)
    fox_out = bf(fox_w_out)

    outs = []
    for b in range(bsz):
        h = x[b]
        for layer in range(depth):
            j = layer // 2
            h = _ffn(h, ffn1_norm[layer], *ffn1, layer)
            if layer % 2 == 0:
                proj, dt_raw = _norm_proj(h, mix_norm[layer], hyb_in, j, hyb_main,
                                          _pad_cols(hyb_in[j, hyb_main:, :].T, LANES), F32)
                o_a = _hgrn(proj, hgrn_lb_logits, hgrn_out_gain[j], j)
                o_b = _ssd(proj, dt_raw, ssd_conv_w[j], ssd_conv_b[j], ssd_dt_bias[j], ssd_A_log[j],
                           ssd_D[j], ssd_out_gain[j], z_blk=4 * hg_width // inner,
                           xs_blk=4 * hg_width // inner + 1)
                h = _out_proj(h, [o_a, o_b], hyb_out, j)
            else:
                qkv, f_logit = _norm_proj(h, mix_norm[layer], fox_in, j, 3 * fox_width,
                                          _pad_cols(fox_in[j, 3 * fox_width:, :].T, LANES), BF16)
                cum = _logf_cumsum(f_logit, jnp.pad(fox_b_f[j], (0, LANES - fox_heads)).reshape(1, LANES),
                                   fox_heads)
                o = _attention(qkv, cum, fox_heads)
                h = _out_proj(h, [o], fox_out, j)
            last = layer == depth - 1
            h = _ffn(h, ffn2_norm[layer], *ffn2, layer, final_gain=final_norm if last else None)
        outs.append(h)
    return jnp.stack(outs, axis=0)
```
